```python
import jax, jax.numpy as jnp
from jax import lax
import numpy as np

D_MODEL = 1024
BATCH = 8
SEQ = 4096
DEPTH = 2

CHUNK = 64
Q_BLOCK = 2 * CHUNK
CONV_WIDTH = 3
D_CONV = D_MODEL // 2
N_HEADS = 8
HEAD_DIM = 64
D_ATTN = N_HEADS * HEAD_DIM
D_FF = 2816
N_EXPERTS = 8
TOP_K = 2
D_FF_EXPERT = 2048
N_DENSE = (DEPTH + 1) // 2
N_MOE = DEPTH // 2
EPS = 1e-6
SPLITS = np.cumsum([D_CONV, D_CONV, D_CONV, D_ATTN, D_ATTN, D_ATTN, D_MODEL]).tolist()
D_IN_PROJ = 3 * D_CONV + 3 * D_ATTN + 2 * D_MODEL

kernel_name = "hybrid_shortconv_stickbreaking_moe_block"


def rmsnorm(x, g):
    xf = x.astype(jnp.float32)
    xf = xf * lax.rsqrt(jnp.mean(xf * xf, axis=-1, keepdims=True) + EPS)
    return (xf * g.astype(jnp.float32)).astype(x.dtype)


def causal_dwconv(u, w):
    s = u.shape[1]
    up = jnp.pad(u, ((0, 0), (CONV_WIDTH - 1, 0), (0, 0)))
    return sum(w[k] * up[:, k:k + s] for k in range(CONV_WIDTH))


def stick_breaking_attention(q, k, v):
    b, s, h, dh = q.shape
    n_blk = s // Q_BLOCK
    scale = dh ** -0.5
    qb = q.reshape(b, n_blk, Q_BLOCK, h, dh).transpose(1, 0, 3, 2, 4)
    kk = k.transpose(0, 2, 1, 3)
    vv = v.transpose(0, 2, 1, 3)
    key_pos = jnp.arange(s)

    def one_block(args):
        qi, blk = args
        q_pos = blk * Q_BLOCK + jnp.arange(Q_BLOCK)
        mask = key_pos[None, :] < q_pos[:, None]
        z = jnp.einsum('bhqd,bhkd->bhqk', qi, kk).astype(jnp.float32) * scale
        log_beta = jax.nn.log_sigmoid(z)
        log_1m = jnp.where(mask, log_beta - z, 0.0)
        incl = lax.cumsum(log_1m, axis=3, reverse=True)
        excl = jnp.concatenate([incl[..., 1:], jnp.zeros_like(incl[..., :1])], axis=-1)
        a = jnp.where(mask, jnp.exp(log_beta + excl), 0.0)
        return jnp.einsum('bhqk,bhkd->bhqd', a.astype(v.dtype), vv)

    o = lax.map(one_block, (qb, jnp.arange(n_blk)))
    return o.transpose(1, 0, 3, 2, 4).reshape(b, s, h * dh)


def swiglu(h, wg, wu, wd):
    return (jax.nn.silu(h @ wg) * (h @ wu)) @ wd


def moe_swiglu(h, w_router, wg, wu, wd):
    logits = h.astype(jnp.float32) @ w_router.astype(jnp.float32)
    top_vals, top_idx = lax.top_k(logits, TOP_K)
    top_w = jax.nn.softmax(top_vals, axis=-1)
    combine = jnp.sum(jax.nn.one_hot(top_idx, N_EXPERTS, dtype=jnp.float32) * top_w[..., None], axis=-2)
    combine = combine.astype(h.dtype)
    out = jnp.zeros_like(h)
    for e in range(N_EXPERTS):
        out = out + combine[..., e:e + 1] * swiglu(h, wg[e], wu[e], wd[e])
    return out


def setup_inputs(seed: int = 0) -> dict:
    key = jax.random.key(seed)
    ks = jax.random.split(key, 17)

    def nrm(k, shape, fan_in):
        return jax.random.normal(k, shape, jnp.float32) * (fan_in ** -0.5)

    def gain(k, shape):
        return 1.0 + 0.02 * jax.random.normal(k, shape, jnp.float32)

    return {
        "x": jax.random.normal(ks[0], (BATCH, SEQ, D_MODEL), jnp.float32),
        "g_mix": gain(ks[1], (DEPTH, D_MODEL)),
        "w_in": nrm(ks[2], (DEPTH, D_MODEL, D_IN_PROJ), D_MODEL),
        "conv_w": nrm(ks[3], (DEPTH, CONV_WIDTH, D_CONV), CONV_WIDTH),
        "w_branch_conv": nrm(ks[4], (DEPTH, D_CONV, D_MODEL), D_CONV),
        "w_branch_attn": nrm(ks[5], (DEPTH, D_ATTN, D_MODEL), D_ATTN),
        "w_out": nrm(ks[6], (DEPTH, D_MODEL, D_MODEL), D_MODEL),
        "g_ffn": gain(ks[7], (DEPTH, D_MODEL)),
        "w_ffn_gate": nrm(ks[8], (N_DENSE, D_MODEL, D_FF), D_MODEL),
        "w_ffn_up": nrm(ks[9], (N_DENSE, D_MODEL, D_FF), D_MODEL),
        "w_ffn_down": nrm(ks[10], (N_DENSE, D_FF, D_MODEL), D_FF),
        "w_router": nrm(ks[11], (N_MOE, D_MODEL, N_EXPERTS), D_MODEL),
        "w_exp_gate": nrm(ks[12], (N_MOE, N_EXPERTS, D_MODEL, D_FF_EXPERT), D_MODEL),
        "w_exp_up": nrm(ks[13], (N_MOE, N_EXPERTS, D_MODEL, D_FF_EXPERT), D_MODEL),
        "w_exp_down": nrm(ks[14], (N_MOE, N_EXPERTS, D_FF_EXPERT, D_MODEL), D_FF_EXPERT),
        "g_final": gain(ks[15], (D_MODEL,)),
    }


def reference(x, g_mix, w_in, conv_w, w_branch_conv, w_branch_attn, w_out, g_ffn,
              w_ffn_gate, w_ffn_up, w_ffn_down, w_router, w_exp_gate, w_exp_up,
              w_exp_down, g_final):
    b, s, _ = x.shape
    for i in range(DEPTH):
        h = rmsnorm(x, g_mix[i])
        proj = h @ w_in[i]
        xin, bg, cg, q, k, v, ga, gb = jnp.split(proj, SPLITS, axis=-1)
        ua = bg * causal_dwconv(cg * xin, conv_w[i])
        ub = stick_breaking_attention(q.reshape(b, s, N_HEADS, HEAD_DIM),
                                      k.reshape(b, s, N_HEADS, HEAD_DIM),
                                      v.reshape(b, s, N_HEADS, HEAD_DIM))
        mix = jax.nn.sigmoid(ga) * (ua @ w_branch_conv[i]) + jax.nn.sigmoid(gb) * (ub @ w_branch_attn[i])
        x = x + mix @ w_out[i]
        h = rmsnorm(x, g_ffn[i])
        j = i // 2
        if i % 2 == 0:
            x = x + swiglu(h, w_ffn_gate[j], w_ffn_up[j], w_ffn_down[j])
        else:
            x = x + moe_swiglu(h, w_router[j], w_exp_gate[j], w_exp_up[j], w_exp_down[j])
    return rmsnorm(x, g_final)
```

```python
import functools

import jax
import jax.numpy as jnp
from jax import lax
from jax.experimental import pallas as pl
from jax.experimental.pallas import tpu as pltpu

F32 = jnp.float32
BF16 = jnp.bfloat16

D_MODEL = 1024
D_CONV = 512
D_ATTN = 512
N_HEADS = 8
HEAD_DIM = 64
N_EXPERTS = 8
TOP_K = 2
EPS = 1e-6
CONV_WIDTH = 3

LANES = 128
VMEM_LIMIT_BYTES = 56 * 1024 * 1024

ATTN_BLOCK = 128
LOG_F32_ZERO = -104.0
ROUTER_PAD = 128


def _params(n_grid, vmem=VMEM_LIMIT_BYTES):
    return pltpu.CompilerParams(
        dimension_semantics=("arbitrary",) * n_grid, vmem_limit_bytes=vmem)


def _rms(x, g):
    return x * lax.rsqrt(jnp.mean(x * x, axis=-1, keepdims=True) + EPS) * g


def _dot(a, b):
    return jnp.dot(a, b, preferred_element_type=F32)


def _mix_in_body(x_ref, g_ref, win_ref, cw_ref, wa_ref,
                 q_ref, k_ref, v_ref, ma_ref, gbs_ref, ubuf, *, tm, tiles_per_seq):
    i = pl.program_id(0)
    h = _rms(x_ref[...], g_ref[...]).astype(BF16)

    p = _dot(h, win_ref[:, 0:3 * D_CONV])
    u = p[:, 2 * D_CONV:3 * D_CONV] * p[:, 0:D_CONV]

    @pl.when(i % tiles_per_seq == 0)
    def _():
        ubuf[0:8, :] = jnp.zeros((8, D_CONV), F32)

    ubuf[8:8 + tm, :] = u
    cw = cw_ref[...]
    conv = (cw[0:1, :] * ubuf[6:6 + tm, :] + cw[1:2, :] * ubuf[7:7 + tm, :]) + cw[2:3, :] * u
    ubuf[0:8, :] = ubuf[tm:tm + 8, :]
    ua = (p[:, D_CONV:2 * D_CONV] * conv).astype(BF16)

    o = 3 * D_CONV
    qkv = _dot(h, win_ref[:, o:o + 3 * D_ATTN])
    q_ref[...] = (qkv[:, 0:D_ATTN] * (HEAD_DIM ** -0.5)).astype(BF16)
    k_ref[...] = qkv[:, D_ATTN:2 * D_ATTN].astype(BF16)
    v_ref[...] = qkv[:, 2 * D_ATTN:3 * D_ATTN].astype(BF16)

    o += 3 * D_ATTN
    ga = _dot(h, win_ref[:, o:o + D_MODEL])
    ma_ref[...] = (jax.nn.sigmoid(ga) * _dot(ua, wa_ref[...])).astype(BF16)
    gb = _dot(h, win_ref[:, o + D_MODEL:o + 2 * D_MODEL])
    gbs_ref[...] = jax.nn.sigmoid(gb).astype(BF16)


def _mix_in(x2, g, win, cw, wa, seq):
    t = x2.shape[0]
    tm = min(512, seq)
    n_in = win.shape[1]
    body = functools.partial(_mix_in_body, tm=tm, tiles_per_seq=seq // tm)
    row = lambda w: pl.BlockSpec((tm, w), lambda i: (i, 0))
    full = lambda a, b: pl.BlockSpec((a, b), lambda i: (0, 0))
    return pl.pallas_call(
        body,
        grid=(t // tm,),
        in_specs=[row(D_MODEL), full(1, D_MODEL), full(D_MODEL, n_in),
                  full(CONV_WIDTH, D_CONV), full(D_CONV, D_MODEL)],
        out_specs=[row(D_ATTN), row(D_ATTN), row(D_ATTN), row(D_MODEL), row(D_MODEL)],
        out_shape=[jax.ShapeDtypeStruct((t, D_ATTN), BF16)] * 3
        + [jax.ShapeDtypeStruct((t, D_MODEL), BF16)] * 2,
        scratch_shapes=[pltpu.VMEM((tm + 8, D_CONV), F32)],
        compiler_params=_params(1),
        name="mix_in",
    )(x2, g, win, cw, wa)


def _attn_body(q_ref, k_ref, v_ref, o_ref, r_scr, o_scr):
    i = pl.program_id(1)
    tb = ATTN_BLOCK
    n_pairs = N_HEADS // 2
    lane = lax.broadcasted_iota(jnp.int32, (tb, LANES), 1)
    first = lane < HEAD_DIM
    row2 = lax.broadcasted_iota(jnp.int32, (2 * tb, tb), 0)
    col2 = lax.broadcasted_iota(jnp.int32, (2 * tb, tb), 1)
    causal2 = col2 < jnp.where(row2 >= tb, row2 - tb, row2)
    rj = lax.broadcasted_iota(jnp.int32, (tb, 2 * tb), 0)
    cs_ = lax.broadcasted_iota(jnp.int32, (tb, 2 * tb), 1)
    suffix = jnp.where(jnp.logical_or(rj > cs_, cs_ >= tb), 1.0, 0.0).astype(BF16)
    zero_b = jnp.zeros((tb, LANES), BF16)

    r_scr[...] = jnp.zeros_like(r_scr)
    o_scr[...] = jnp.zeros_like(o_scr)

    def visit(j, diag):
        start = pl.multiple_of(j * tb, tb)
        rmax = None
        for p in range(n_pairs):
            sl = slice(p * LANES, (p + 1) * LANES)
            qp = q_ref[0, :, sl]
            kp = k_ref[0, pl.ds(start, tb), sl]
            vp = v_ref[0, pl.ds(start, tb), sl]
            q2 = jnp.concatenate([jnp.where(first, qp, zero_b), jnp.where(first, zero_b, qp)], axis=0)
            z = lax.dot_general(q2, kp, (((1,), (1,)), ((), ())), preferred_element_type=F32)
            sp = jnp.log(1.0 + jnp.exp(-jnp.abs(z)))
            log_beta = jnp.minimum(z, 0.0) - sp
            log_1m = log_beta - z
            if diag:
                log_1m = jnp.where(causal2, log_1m, 0.0)
            c = _dot(log_1m.astype(BF16), suffix)
            r_old = r_scr[p]
            a = jnp.exp(log_beta + c[:, 0:tb] + r_old)
            if diag:
                a = jnp.where(causal2, a, 0.0)
            r_new = r_old + c[:, tb:2 * tb]
            r_scr[p] = r_new
            a2 = jnp.concatenate([a[0:tb], a[tb:2 * tb]], axis=1).astype(BF16)
            v2 = jnp.concatenate([jnp.where(first, vp, zero_b), jnp.where(first, zero_b, vp)], axis=0)
            o_scr[p] += _dot(a2, v2)
            rmax = r_new if rmax is None else jnp.maximum(rmax, r_new)
        m = jnp.max(rmax, axis=0, keepdims=True)
        return m[0, 0]

    m0 = visit(i, True)

    def cond(c):
        j, m = c
        return jnp.logical_and(j >= 0, m > LOG_F32_ZERO)

    def body(c):
        j, _ = c
        return j - 1, visit(j, False)

    lax.while_loop(cond, body, (i - 1, m0))

    for p in range(n_pairs):
        o_ref[0, :, p * LANES:(p + 1) * LANES] = o_scr[p].astype(o_ref.dtype)


def _attention(q, k, v):
    b, s, _ = q.shape
    tb = ATTN_BLOCK
    return pl.pallas_call(
        _attn_body,
        grid=(b, s // tb),
        in_specs=[pl.BlockSpec((1, tb, D_ATTN), lambda bi, i: (bi, i, 0)),
                  pl.BlockSpec((1, s, D_ATTN), lambda bi, i: (bi, 0, 0)),
                  pl.BlockSpec((1, s, D_ATTN), lambda bi, i: (bi, 0, 0))],
        out_specs=pl.BlockSpec((1, tb, D_ATTN), lambda bi, i: (bi, i, 0)),
        out_shape=jax.ShapeDtypeStruct((b, s, D_ATTN), BF16),
        scratch_shapes=[pltpu.VMEM((N_HEADS // 2, 2 * tb, tb), F32),
                        pltpu.VMEM((N_HEADS // 2, tb, LANES), F32)],
        compiler_params=_params(2),
        name="stickbreak_attn",
    )(q, k, v)


def _mix_out_body(*refs, router):
    if router:
        (x_ref, ma_ref, gbs_ref, ub_ref, wb_ref, wo_ref, g_ref, wr_ref,
         xo_ref, h_ref, lg_ref) = refs
    else:
        x_ref, ma_ref, gbs_ref, ub_ref, wb_ref, wo_ref, g_ref, xo_ref, h_ref = refs
    mix = ma_ref[...].astype(F32) + gbs_ref[...].astype(F32) * _dot(ub_ref[...], wb_ref[...])
    xn = x_ref[...] + _dot(mix.astype(BF16), wo_ref[...])
    xo_ref[...] = xn
    h = _rms(xn, g_ref[...])
    h_ref[...] = h.astype(h_ref.dtype)
    if router:
        lg_ref[...] = jnp.dot(h, wr_ref[...], preferred_element_type=F32,
                              precision=lax.Precision.HIGHEST)


def _mix_out(x2, ma, gbs, ub, wb, wo, g, wr=None):
    t = x2.shape[0]
    tm = min(512, t)
    router = wr is not None
    row = lambda w: pl.BlockSpec((tm, w), lambda i: (i, 0))
    full = lambda a, b: pl.BlockSpec((a, b), lambda i: (0, 0))
    in_specs = [row(D_MODEL), row(D_MODEL), row(D_MODEL), row(D_ATTN),
                full(D_ATTN, D_MODEL), full(D_MODEL, D_MODEL), full(1, D_MODEL)]
    out_specs = [row(D_MODEL), row(D_MODEL)]
    out_shape = [jax.ShapeDtypeStruct((t, D_MODEL), F32),
                 jax.ShapeDtypeStruct((t, D_MODEL), F32 if router else BF16)]
    args = [x2, ma, gbs, ub, wb, wo, g]
    if router:
        in_specs.append(full(D_MODEL, ROUTER_PAD))
        out_specs.append(row(ROUTER_PAD))
        out_shape.append(jax.ShapeDtypeStruct((t, ROUTER_PAD), F32))
        args.append(wr)
    return pl.pallas_call(
        functools.partial(_mix_out_body, router=router),
        grid=(t // tm,),
        in_specs=in_specs, out_specs=out_specs, out_shape=out_shape,
        compiler_params=_params(1),
        name="mix_out_router" if router else "mix_out",
    )(*args)


def _ffn_body(x_ref, h_ref, wg_ref, wu_ref, wd_ref, o_ref):
    h = h_ref[...]
    act = (jax.nn.silu(_dot(h, wg_ref[...])) * _dot(h, wu_ref[...])).astype(BF16)
    o_ref[...] = x_ref[...] + _dot(act, wd_ref[...])


def _dense_ffn(x2, h, wg, wu, wd):
    t = x2.shape[0]
    tm = min(256, t)
    dff = wg.shape[1]
    row = lambda w: pl.BlockSpec((tm, w), lambda i: (i, 0))
    full = lambda a, b: pl.BlockSpec((a, b), lambda i: (0, 0))
    return pl.pallas_call(
        _ffn_body,
        grid=(t // tm,),
        in_specs=[row(D_MODEL), row(D_MODEL), full(D_MODEL, dff), full(D_MODEL, dff),
                  full(dff, D_MODEL)],
        out_specs=row(D_MODEL),
        out_shape=jax.ShapeDtypeStruct((t, D_MODEL), F32),
        compiler_params=_params(1),
        name="dense_ffn",
    )(x2, h, wg, wu, wd)


def _route(logits, tm):
    t = logits.shape[0]
    top_vals, top_idx = lax.top_k(logits, TOP_K)
    top_w = jax.nn.softmax(top_vals, axis=-1)
    e_flat = top_idx.reshape(-1).astype(jnp.int32)
    onehot = (e_flat[:, None] == jnp.arange(N_EXPERTS, dtype=jnp.int32)[None, :]).astype(jnp.int32)
    csum = jnp.cumsum(onehot, axis=0)
    counts = csum[-1]
    rank = jnp.sum(csum * onehot, axis=1) - 1
    padded = ((counts + tm - 1) // tm) * tm
    ends = jnp.cumsum(padded)
    starts = ends - padded
    pos = (jnp.sum(starts[None, :] * onehot, axis=1) + rank).astype(jnp.int32)
    n_tiles = (TOP_K * t) // tm + N_EXPERTS
    tile_start = jnp.arange(n_tiles, dtype=jnp.int32) * tm
    tile_expert = jnp.sum((tile_start[:, None] >= ends[None, :]).astype(jnp.int32), axis=1)
    tile_expert = jnp.minimum(tile_expert, N_EXPERTS - 1).astype(jnp.int32)
    n_valid = (ends[-1] // tm).astype(jnp.int32).reshape(1)
    return top_w, pos, tile_expert, n_valid, n_tiles


def _dispatch_body(pos_ref, h_ref, xs_in_ref, xs_ref, sem, *, tm):
    del xs_in_ref
    base = pl.program_id(0) * (TOP_K * tm)

    def copy(r, k):
        dst = pos_ref[base + TOP_K * r + k]
        return pltpu.make_async_copy(h_ref.at[pl.ds(r, 1)], xs_ref.at[pl.ds(dst, 1)], sem.at[k])

    def start(r, c):
        for k in range(TOP_K):
            copy(r, k).start()
        return c

    def wait(r, c):
        for k in range(TOP_K):
            copy(r, k).wait()
        return c

    lax.fori_loop(0, tm, start, 0)
    lax.fori_loop(0, tm, wait, 0)


def _dispatch(pos, h, n_rows):
    t = h.shape[0]
    tm = min(256, t)
    xs0 = jnp.zeros((n_rows, D_MODEL), F32)
    return pl.pallas_call(
        functools.partial(_dispatch_body, tm=tm),
        grid_spec=pltpu.PrefetchScalarGridSpec(
            num_scalar_prefetch=1,
            grid=(t // tm,),
            in_specs=[pl.BlockSpec((tm, D_MODEL), lambda i, pos: (i, 0)),
                      pl.BlockSpec(memory_space=pl.ANY)],
            out_specs=pl.BlockSpec(memory_space=pl.ANY),
            scratch_shapes=[pltpu.SemaphoreType.DMA((TOP_K,))],
        ),
        out_shape=jax.ShapeDtypeStruct((n_rows, D_MODEL), F32),
        input_output_aliases={2: 0},
        compiler_params=_params(1),
        name="moe_dispatch",
    )(pos, h, xs0)


def _expert_body(te_ref, nv_ref, xs_ref, wg_ref, wu_ref, wd_ref, y_ref):
    del te_ref
    i = pl.program_id(0)

    @pl.when(i < nv_ref[0])
    def _():
        x = xs_ref[...].astype(BF16)
        act = (jax.nn.silu(_dot(x, wg_ref[0])) * _dot(x, wu_ref[0])).astype(BF16)
        y_ref[...] = _dot(act, wd_ref[0])

    @pl.when(i >= nv_ref[0])
    def _():
        y_ref[...] = jnp.zeros_like(y_ref)


def _experts(tile_expert, n_valid, xs, wg, wu, wd, tm):
    n_rows = xs.shape[0]
    dfe = wg.shape[2]
    return pl.pallas_call(
        _expert_body,
        grid_spec=pltpu.PrefetchScalarGridSpec(
            num_scalar_prefetch=2,
            grid=(n_rows // tm,),
            in_specs=[pl.BlockSpec((tm, D_MODEL), lambda i, te, nv: (i, 0)),
                      pl.BlockSpec((1, D_MODEL, dfe), lambda i, te, nv: (te[i], 0, 0)),
                      pl.BlockSpec((1, D_MODEL, dfe), lambda i, te, nv: (te[i], 0, 0)),
                      pl.BlockSpec((1, dfe, D_MODEL), lambda i, te, nv: (te[i], 0, 0))],
            out_specs=pl.BlockSpec((tm, D_MODEL), lambda i, te, nv: (i, 0)),
        ),
        out_shape=jax.ShapeDtypeStruct((n_rows, D_MODEL), F32),
        compiler_params=_params(1),
        name="moe_experts",
    )(tile_expert, n_valid, xs, wg, wu, wd)


def _combine_body(pos_ref, x_ref, w_ref, g_ref, y_ref, o_ref, ybuf, sem, *, tm, final_norm):
    base = pl.program_id(0) * (TOP_K * tm)

    def copy(r, k):
        src = pos_ref[base + TOP_K * r + k]
        return pltpu.make_async_copy(y_ref.at[pl.ds(src, 1)], ybuf.at[k, pl.ds(r, 1)], sem.at[k])

    def start(r, c):
        for k in range(TOP_K):
            copy(r, k).start()
        return c

    def wait(r, c):
        for k in range(TOP_K):
            copy(r, k).wait()
        return c

    lax.fori_loop(0, tm, start, 0)
    lax.fori_loop(0, tm, wait, 0)
    w = w_ref[...]
    xn = x_ref[...] + (w[:, 0:1] * ybuf[0] + w[:, 1:2] * ybuf[1])
    o_ref[...] = _rms(xn, g_ref[...]) if final_norm else xn


def _combine(pos, x2, top_w, g, y, final_norm):
    t = x2.shape[0]
    tm = min(256, t)
    return pl.pallas_call(
        functools.partial(_combine_body, tm=tm, final_norm=final_norm),
        grid_spec=pltpu.PrefetchScalarGridSpec(
            num_scalar_prefetch=1,
            grid=(t // tm,),
            in_specs=[pl.BlockSpec((tm, D_MODEL), lambda i, pos: (i, 0)),
                      pl.BlockSpec((tm, TOP_K), lambda i, pos: (i, 0)),
                      pl.BlockSpec((1, D_MODEL), lambda i, pos: (0, 0)),
                      pl.BlockSpec(memory_space=pl.ANY)],
            out_specs=pl.BlockSpec((tm, D_MODEL), lambda i, pos: (i, 0)),
            scratch_shapes=[pltpu.VMEM((TOP_K, tm, D_MODEL), F32),
                            pltpu.SemaphoreType.DMA((TOP_K,))],
        ),
        out_shape=jax.ShapeDtypeStruct((t, D_MODEL), F32),
        compiler_params=_params(1),
        name="moe_combine",
    )(pos, x2, top_w, g, y)


def _final_norm_body(x_ref, g_ref, o_ref):
    o_ref[...] = _rms(x_ref[...], g_ref[...])


def _final_norm(x2, g):
    t = x2.shape[0]
    tm = min(512, t)
    return pl.pallas_call(
        _final_norm_body,
        grid=(t // tm,),
        in_specs=[pl.BlockSpec((tm, D_MODEL), lambda i: (i, 0)),
                  pl.BlockSpec((1, D_MODEL), lambda i: (0, 0))],
        out_specs=pl.BlockSpec((tm, D_MODEL), lambda i: (i, 0)),
        out_shape=jax.ShapeDtypeStruct((t, D_MODEL), F32),
        compiler_params=_params(1),
        name="final_norm",
    )(x2, g)


def kernel(x, g_mix, w_in, conv_w, w_branch_conv, w_branch_attn, w_out, g_ffn, w_ffn_gate, w_ffn_up, w_ffn_down, w_router, w_exp_gate, w_exp_up, w_exp_down, g_final):
    b, s, d = x.shape
    depth = g_mix.shape[0]
    t = b * s
    x2 = x.reshape(t, d)
    gf = g_final.reshape(1, d)
    for i in range(depth):
        last = i == depth - 1
        moe = i % 2 == 1
        j = i // 2
        q, k, v, ma, gbs = _mix_in(x2, g_mix[i].reshape(1, d), w_in[i].astype(BF16), conv_w[i],
                                   w_branch_conv[i].astype(BF16), s)
        ub = _attention(q.reshape(b, s, D_ATTN), k.reshape(b, s, D_ATTN), v.reshape(b, s, D_ATTN))
        ub = ub.reshape(t, D_ATTN)
        wb = w_branch_attn[i].astype(BF16)
        wo = w_out[i].astype(BF16)
        gi = g_ffn[i].reshape(1, d)
        if not moe:
            x2, h = _mix_out(x2, ma, gbs, ub, wb, wo, gi)
            x2 = _dense_ffn(x2, h, w_ffn_gate[j].astype(BF16), w_ffn_up[j].astype(BF16),
                            w_ffn_down[j].astype(BF16))
            if last:
                x2 = _final_norm(x2, gf)
        else:
            wr = jnp.pad(w_router[j], ((0, 0), (0, ROUTER_PAD - N_EXPERTS)))
            x2, h, logits = _mix_out(x2, ma, gbs, ub, wb, wo, gi, wr)
            tm = min(512, t)
            top_w, pos, tile_expert, n_valid, n_tiles = _route(logits[:, :N_EXPERTS], tm)
            xs = _dispatch(pos, h, n_tiles * tm)
            y = _experts(tile_expert, n_valid, xs, w_exp_gate[j].astype(BF16),
                         w_exp_up[j].astype(BF16), w_exp_down[j].astype(BF16), tm)
            x2 = _combine(pos, x2, top_w, gf, y, final_norm=last)
    return x2.reshape(b, s, d)
```

```python
import functools
import math

import jax
import jax.numpy as jnp
from jax import lax
from jax.experimental import pallas as pl
from jax.experimental.pallas import tpu as pltpu

F32 = jnp.float32
BF16 = jnp.bfloat16

D_MODEL = 1024
D_CONV = 512
D_ATTN = 512
N_HEADS = 8
HEAD_DIM = 64
N_EXPERTS = 8
TOP_K = 2
EPS = 1e-6
CONV_WIDTH = 3

LANES = 128
VMEM_LIMIT_BYTES = 56 * 1024 * 1024

ATTN_BLOCK = 128
ATTN_GROUP = 2
N_PAIRS = N_HEADS // 2
D_SPLIT = 2 * D_ATTN
LOG2_F32_ZERO = -151.0
ROUTER_PAD = 128


def _params(n_grid, vmem=VMEM_LIMIT_BYTES):
    return pltpu.CompilerParams(
        dimension_semantics=("arbitrary",) * n_grid, vmem_limit_bytes=vmem)


def _rms(x, g):
    return x * lax.rsqrt(jnp.mean(x * x, axis=-1, keepdims=True) + EPS) * g


def _dot(a, b):
    return jnp.dot(a, b, preferred_element_type=F32)


def _mix_in_body(x_ref, g_ref, win_ref, cw_ref, wa_ref,
                 qz_ref, k_ref, vz_ref, ma_ref, gbs_ref, ubuf, *, tm, tiles_per_seq):
    i = pl.program_id(0)
    h = _rms(x_ref[...], g_ref[...]).astype(BF16)

    p = _dot(h, win_ref[:, 0:3 * D_CONV])
    u = p[:, 2 * D_CONV:3 * D_CONV] * p[:, 0:D_CONV]

    @pl.when(i % tiles_per_seq == 0)
    def _():
        ubuf[0:8, :] = jnp.zeros((8, D_CONV), F32)

    ubuf[8:8 + tm, :] = u
    cw = cw_ref[...]
    conv = (cw[0:1, :] * ubuf[6:6 + tm, :] + cw[1:2, :] * ubuf[7:7 + tm, :]) + cw[2:3, :] * u
    ubuf[0:8, :] = ubuf[tm:tm + 8, :]
    ua = (p[:, D_CONV:2 * D_CONV] * conv).astype(BF16)

    o = 3 * D_CONV
    qkv = _dot(h, win_ref[:, o:o + 3 * D_ATTN])
    first = lax.broadcasted_iota(jnp.int32, (tm, LANES), 1) < HEAD_DIM
    q_scale = (HEAD_DIM ** -0.5) * math.log2(math.e)
    for pr in range(N_PAIRS):
        qs = qkv[:, pr * LANES:(pr + 1) * LANES] * q_scale
        vs = qkv[:, 2 * D_ATTN + pr * LANES:2 * D_ATTN + (pr + 1) * LANES]
        lo, mid, hi = 2 * pr * LANES, (2 * pr + 1) * LANES, (2 * pr + 2) * LANES
        qz_ref[:, lo:mid] = jnp.where(first, qs, 0.0).astype(BF16)
        qz_ref[:, mid:hi] = jnp.where(first, 0.0, qs).astype(BF16)
        vz_ref[:, lo:mid] = jnp.where(first, vs, 0.0).astype(BF16)
        vz_ref[:, mid:hi] = jnp.where(first, 0.0, vs).astype(BF16)
    k_ref[...] = qkv[:, D_ATTN:2 * D_ATTN].astype(BF16)

    o += 3 * D_ATTN
    ga = _dot(h, win_ref[:, o:o + D_MODEL])
    ma_ref[...] = (jax.nn.sigmoid(ga) * _dot(ua, wa_ref[...])).astype(BF16)
    gb = _dot(h, win_ref[:, o + D_MODEL:o + 2 * D_MODEL])
    gbs_ref[...] = jax.nn.sigmoid(gb).astype(BF16)


def _mix_in(x2, g, win, cw, wa, seq):
    t = x2.shape[0]
    tm = min(512, seq)
    n_in = win.shape[1]
    body = functools.partial(_mix_in_body, tm=tm, tiles_per_seq=seq // tm)
    row = lambda w: pl.BlockSpec((tm, w), lambda i: (i, 0))
    full = lambda a, b: pl.BlockSpec((a, b), lambda i: (0, 0))
    return pl.pallas_call(
        body,
        grid=(t // tm,),
        in_specs=[row(D_MODEL), full(1, D_MODEL), full(D_MODEL, n_in),
                  full(CONV_WIDTH, D_CONV), full(D_CONV, D_MODEL)],
        out_specs=[row(D_SPLIT), row(D_ATTN), row(D_SPLIT), row(D_MODEL), row(D_MODEL)],
        out_shape=[jax.ShapeDtypeStruct((t, D_SPLIT), BF16),
                   jax.ShapeDtypeStruct((t, D_ATTN), BF16),
                   jax.ShapeDtypeStruct((t, D_SPLIT), BF16),
                   jax.ShapeDtypeStruct((t, D_MODEL), BF16),
                   jax.ShapeDtypeStruct((t, D_MODEL), BF16)],
        scratch_shapes=[pltpu.VMEM((tm + 8, D_CONV), F32)],
        compiler_params=_params(1),
        name="mix_in",
    )(x2, g, win, cw, wa)


def _attn_body(qz_ref, k_ref, vz_ref, o_ref, r_scr, o_scr):
    i = pl.program_id(1)
    tb = ATTN_BLOCK
    row2 = lax.broadcasted_iota(jnp.int32, (2 * tb, tb), 0)
    col2 = lax.broadcasted_iota(jnp.int32, (2 * tb, tb), 1)
    causal2 = col2 < jnp.where(row2 >= tb, row2 - tb, row2)
    rj = lax.broadcasted_iota(jnp.int32, (tb, 2 * tb), 0)
    cs_ = lax.broadcasted_iota(jnp.int32, (tb, 2 * tb), 1)
    suffix = jnp.where(jnp.logical_or(rj > cs_, cs_ >= tb), 1.0, 0.0).astype(BF16)

    r_scr[...] = jnp.zeros_like(r_scr)
    o_scr[...] = jnp.zeros_like(o_scr)

    def visit(n, diag, groups):
        chains = [(g, p) for g in groups for p in range(N_PAIRS)]
        start = {g: pl.multiple_of((ATTN_GROUP * i + g - n) * tb, tb) for g in groups}
        zs = []
        for g, p in chains:
            rows = slice(g * tb, (g + 1) * tb)
            q2 = jnp.concatenate([qz_ref[0, rows, 2 * p * LANES:(2 * p + 1) * LANES],
                                  qz_ref[0, rows, (2 * p + 1) * LANES:(2 * p + 2) * LANES]], axis=0)
            kp = k_ref[0, pl.ds(start[g], tb), p * LANES:(p + 1) * LANES]
            zs.append(lax.dot_general(q2, kp, (((1,), (1,)), ((), ())),
                                      preferred_element_type=F32))
        log_betas, log_1ms = [], []
        for z in zs:
            sp = jnp.log2(1.0 + jnp.exp2(-jnp.abs(z)))
            log_beta = jnp.minimum(z, 0.0) - sp
            log_1m = log_beta - z
            if diag:
                log_1m = jnp.where(causal2, log_1m, 0.0)
            log_betas.append(log_beta)
            log_1ms.append(log_1m.astype(BF16))
        c_all = _dot(jnp.concatenate(log_1ms, axis=0), suffix)
        rmax = None
        for ci, (g, p) in enumerate(chains):
            idx = g * N_PAIRS + p
            c = c_all[ci * 2 * tb:(ci + 1) * 2 * tb]
            r_old = r_scr[idx]
            a = jnp.exp2(log_betas[ci] + c[:, 0:tb] + r_old)
            if diag:
                a = jnp.where(causal2, a, 0.0)
            r_new = r_old + c[:, tb:2 * tb]
            r_scr[idx] = r_new
            rmax = r_new if rmax is None else jnp.maximum(rmax, r_new)
            a2 = jnp.concatenate([a[0:tb], a[tb:2 * tb]], axis=1).astype(BF16)
            v2 = jnp.concatenate(
                [vz_ref[0, pl.ds(start[g], tb), 2 * p * LANES:(2 * p + 1) * LANES],
                 vz_ref[0, pl.ds(start[g], tb), (2 * p + 1) * LANES:(2 * p + 2) * LANES]], axis=0)
            o_scr[idx] += _dot(a2, v2)
        return jnp.max(rmax, axis=0, keepdims=True)[0, 0]

    all_groups = list(range(ATTN_GROUP))
    m0 = visit(0, True, all_groups)

    def cond(c):
        n, m = c
        return jnp.logical_and(n <= ATTN_GROUP * i, m > LOG2_F32_ZERO)

    def body(c):
        n, _ = c
        return n + 1, visit(n, False, all_groups)

    n_end, m_end = lax.while_loop(cond, body, (jnp.int32(1), m0))

    for extra in range(1, ATTN_GROUP):
        @pl.when(jnp.logical_and(n_end == ATTN_GROUP * i + 1, m_end > LOG2_F32_ZERO))
        def _():
            visit(ATTN_GROUP * i + extra, False, all_groups[extra:])

    for g in range(ATTN_GROUP):
        for p in range(N_PAIRS):
            o_ref[0, g * tb:(g + 1) * tb, p * LANES:(p + 1) * LANES] = (
                o_scr[g * N_PAIRS + p].astype(o_ref.dtype))


def _attention(qz, k, vz):
    b, s, _ = k.shape
    tb = ATTN_BLOCK
    tq = ATTN_GROUP * tb
    n_chains = ATTN_GROUP * N_PAIRS
    return pl.pallas_call(
        _attn_body,
        grid=(b, s // tq),
        in_specs=[pl.BlockSpec((1, tq, D_SPLIT), lambda bi, i: (bi, i, 0)),
                  pl.BlockSpec((1, s, D_ATTN), lambda bi, i: (bi, 0, 0)),
                  pl.BlockSpec((1, s, D_SPLIT), lambda bi, i: (bi, 0, 0))],
        out_specs=pl.BlockSpec((1, tq, D_ATTN), lambda bi, i: (bi, i, 0)),
        out_shape=jax.ShapeDtypeStruct((b, s, D_ATTN), BF16),
        scratch_shapes=[pltpu.VMEM((n_chains, 2 * tb, tb), F32),
                        pltpu.VMEM((n_chains, tb, LANES), F32)],
        compiler_params=_params(2),
        name="stickbreak_attn",
    )(qz, k, vz)


def _mix_out_body(*refs, router):
    if router:
        (x_ref, ma_ref, gbs_ref, ub_ref, wb_ref, wo_ref, g_ref, wr_ref,
         xo_ref, h_ref, lg_ref) = refs
    else:
        x_ref, ma_ref, gbs_ref, ub_ref, wb_ref, wo_ref, g_ref, xo_ref, h_ref = refs
    mix = ma_ref[...].astype(F32) + gbs_ref[...].astype(F32) * _dot(ub_ref[...], wb_ref[...])
    xn = x_ref[...] + _dot(mix.astype(BF16), wo_ref[...])
    xo_ref[...] = xn
    h = _rms(xn, g_ref[...])
    h_ref[...] = h.astype(h_ref.dtype)
    if router:
        h_hi = h.astype(BF16)
        h_lo = (h - h_hi.astype(F32)).astype(BF16)
        c = _dot(h_hi, wr_ref[...])
        lg_ref[...] = (c[:, 0:ROUTER_PAD] + c[:, ROUTER_PAD:2 * ROUTER_PAD]
                       + _dot(h_lo, wr_ref[:, 0:ROUTER_PAD]))


def _mix_out(x2, ma, gbs, ub, wb, wo, g, wr=None):
    t = x2.shape[0]
    tm = min(512, t)
    router = wr is not None
    row = lambda w: pl.BlockSpec((tm, w), lambda i: (i, 0))
    full = lambda a, b: pl.BlockSpec((a, b), lambda i: (0, 0))
    in_specs = [row(D_MODEL), row(D_MODEL), row(D_MODEL), row(D_ATTN),
                full(D_ATTN, D_MODEL), full(D_MODEL, D_MODEL), full(1, D_MODEL)]
    out_specs = [row(D_MODEL), row(D_MODEL)]
    out_shape = [jax.ShapeDtypeStruct((t, D_MODEL), F32),
                 jax.ShapeDtypeStruct((t, D_MODEL), F32 if router else BF16)]
    args = [x2, ma, gbs, ub, wb, wo, g]
    if router:
        in_specs.append(full(D_MODEL, 2 * ROUTER_PAD))
        out_specs.append(row(ROUTER_PAD))
        out_shape.append(jax.ShapeDtypeStruct((t, ROUTER_PAD), F32))
        args.append(wr)
    return pl.pallas_call(
        functools.partial(_mix_out_body, router=router),
        grid=(t // tm,),
        in_specs=in_specs, out_specs=out_specs, out_shape=out_shape,
        compiler_params=_params(1),
        name="mix_out_router" if router else "mix_out",
    )(*args)


def _ffn_body(x_ref, h_ref, wg_ref, wu_ref, wd_ref, o_ref):
    h = h_ref[...]
    act = (jax.nn.silu(_dot(h, wg_ref[...])) * _dot(h, wu_ref[...])).astype(BF16)
    o_ref[...] = x_ref[...] + _dot(act, wd_ref[...])


def _dense_ffn(x2, h, wg, wu, wd):
    t = x2.shape[0]
    tm = min(256, t)
    dff = wg.shape[1]
    row = lambda w: pl.BlockSpec((tm, w), lambda i: (i, 0))
    full = lambda a, b: pl.BlockSpec((a, b), lambda i: (0, 0))
    return pl.pallas_call(
        _ffn_body,
        grid=(t // tm,),
        in_specs=[row(D_MODEL), row(D_MODEL), full(D_MODEL, dff), full(D_MODEL, dff),
                  full(dff, D_MODEL)],
        out_specs=row(D_MODEL),
        out_shape=jax.ShapeDtypeStruct((t, D_MODEL), F32),
        compiler_params=_params(1),
        name="dense_ffn",
    )(x2, h, wg, wu, wd)


def _route(logits, tm):
    t = logits.shape[0]
    top_vals, top_idx = lax.top_k(logits, TOP_K)
    top_w = jax.nn.softmax(top_vals, axis=-1)
    e_flat = top_idx.T.reshape(-1).astype(jnp.int32)
    experts = jnp.arange(N_EXPERTS, dtype=jnp.int32)
    onehot = (e_flat[:, None] == experts[None, :]).astype(jnp.int32)
    csum = jnp.cumsum(onehot, axis=0)
    counts = csum[-1]
    starts = jnp.cumsum(counts) - counts
    pos = jnp.sum((csum - 1 + starts[None, :]) * onehot, axis=1).astype(jnp.int32)
    order = _invert_permutation(pos)
    tiles_e = (counts + tm - 1) // tm
    tile_end = jnp.cumsum(tiles_e)
    tile_begin = tile_end - tiles_e
    n_tiles = (TOP_K * t) // tm + N_EXPERTS
    ti = jnp.arange(n_tiles, dtype=jnp.int32)
    tile_expert = jnp.minimum(
        jnp.sum((ti[:, None] >= tile_end[None, :]).astype(jnp.int32), axis=1), N_EXPERTS - 1)
    local = ti - tile_begin[tile_expert]
    row0 = starts[tile_expert] + local * tm
    n_rows = jnp.clip(counts[tile_expert] - local * tm, 0, tm)
    r = jnp.arange(tm, dtype=jnp.int32)[None, :]
    real = r < n_rows[:, None]
    a = order[jnp.minimum(row0[:, None] + r, TOP_K * t - 1)]
    tok = jnp.where(real, jnp.where(a >= t, a - t, a), 0)
    dst = jnp.where(real, a, TOP_K * t + (ti[:, None] % 2) * tm + r)
    tok_bits = _token_bits(t)
    packed = lax.bitcast_convert_type(
        tok.astype(jnp.uint32) | (dst.astype(jnp.uint32) << tok_bits), jnp.int32).reshape(-1)
    n_valid = tile_end[-1].astype(jnp.int32).reshape(1)
    return top_w, packed, tile_expert.astype(jnp.int32), n_valid, n_tiles


def _token_bits(t):
    bits = max(1, (t - 1).bit_length())
    assert bits + (TOP_K * t + 2 * t - 1).bit_length() <= 32, "row index does not pack in 32 bits"
    return bits


def _invert_body(pos_ref, order_ref, *, n):
    def body(a, c):
        order_ref[pos_ref[a]] = a
        return c
    lax.fori_loop(0, n, body, 0, unroll=16)


def _invert_permutation(pos):
    n = pos.shape[0]
    return pl.pallas_call(
        functools.partial(_invert_body, n=n),
        in_specs=[pl.BlockSpec(memory_space=pltpu.SMEM)],
        out_specs=pl.BlockSpec(memory_space=pltpu.SMEM),
        out_shape=jax.ShapeDtypeStruct((n,), jnp.int32),
        name="moe_invert",
    )(pos)


def _expert_body(te_ref, nv_ref, idx_ref, h_ref, wg_ref, wu_ref, wd_ref,
                 slots_ref, xbuf, ybuf, gsem, ssem, *, tm, t):
    del te_ref
    i = pl.program_id(0)
    n_valid = nv_ref[0]
    slot = i % 2
    tok_bits = _token_bits(t)
    tok_mask = (1 << tok_bits) - 1

    def row_loop(tile, issue):
        base = tile * tm

        def body(g, c):
            r8 = pl.multiple_of(g * 8, 8)
            for j in range(8):
                issue(idx_ref[base + r8 + j], r8, j)
            return c
        lax.fori_loop(0, tm // 8, body, 0)

    def gather_start(tile, sl):
        def issue(packed, r8, j):
            tok = jnp.bitwise_and(packed, tok_mask)
            pltpu.make_async_copy(h_ref.at[pl.ds(tok, 1)],
                                  xbuf.at[sl, pl.ds(r8, 8)].at[pl.ds(j, 1)], gsem.at[sl]).start()
        row_loop(tile, issue)

    def gather_wait(sl):
        pltpu.make_async_copy(h_ref.at[pl.ds(0, tm)], xbuf.at[sl], gsem.at[sl]).wait()

    def scatter_start(tile, sl):
        def issue(packed, r8, j):
            dst = lax.shift_right_logical(packed, tok_bits)
            pltpu.make_async_copy(ybuf.at[sl, pl.ds(r8, 8)].at[pl.ds(j, 1)],
                                  slots_ref.at[pl.ds(dst, 1)], ssem.at[sl]).start()
        row_loop(tile, issue)

    def scatter_wait(sl):
        pltpu.make_async_copy(ybuf.at[sl], slots_ref.at[pl.ds(0, tm)], ssem.at[sl]).wait()

    @pl.when(i == 0)
    def _():
        ybuf[0] = jnp.zeros((tm, D_MODEL), F32)
        for sl in range(2):
            dump = pltpu.make_async_copy(
                ybuf.at[0], slots_ref.at[pl.ds(TOP_K * t + sl * tm, tm)], ssem.at[0])
            dump.start()
            dump.wait()

    @pl.when(jnp.logical_and(i == 0, n_valid > 0))
    def _():
        gather_start(0, 0)

    @pl.when(i + 1 < n_valid)
    def _():
        gather_start(i + 1, 1 - slot)

    @pl.when(i < n_valid)
    def _():
        gather_wait(slot)

        @pl.when(i >= 2)
        def _():
            scatter_wait(slot)

        x = xbuf[slot].astype(BF16)
        act = (jax.nn.silu(_dot(x, wg_ref[0])) * _dot(x, wu_ref[0])).astype(BF16)
        ybuf[slot] = _dot(act, wd_ref[0])
        scatter_start(i, slot)

        @pl.when(i == n_valid - 1)
        def _():
            @pl.when(i >= 1)
            def _():
                scatter_wait(1 - slot)
            scatter_wait(slot)


def _experts(tile_expert, n_valid, packed, h, wg, wu, wd, tm, n_tiles):
    t = h.shape[0]
    dfe = wg.shape[2]
    wmap = lambda i, te, nv, ix: (te[i], 0, 0)
    return pl.pallas_call(
        functools.partial(_expert_body, tm=tm, t=t),
        grid_spec=pltpu.PrefetchScalarGridSpec(
            num_scalar_prefetch=3,
            grid=(n_tiles,),
            in_specs=[pl.BlockSpec(memory_space=pl.ANY),
                      pl.BlockSpec((1, D_MODEL, dfe), wmap),
                      pl.BlockSpec((1, D_MODEL, dfe), wmap),
                      pl.BlockSpec((1, dfe, D_MODEL), wmap)],
            out_specs=pl.BlockSpec(memory_space=pl.ANY),
            scratch_shapes=[pltpu.VMEM((2, tm, D_MODEL), F32),
                            pltpu.VMEM((2, tm, D_MODEL), F32),
                            pltpu.SemaphoreType.DMA((2,)),
                            pltpu.SemaphoreType.DMA((2,))],
        ),
        out_shape=jax.ShapeDtypeStruct((TOP_K * t + 2 * tm, D_MODEL), F32),
        compiler_params=_params(1),
        name="moe_experts",
    )(tile_expert, n_valid, packed, h, wg, wu, wd)


def _combine_body(x_ref, s0_ref, s1_ref, w_ref, g_ref, o_ref, *, final_norm):
    w = w_ref[...]
    xn = x_ref[...] + (w[:, 0:1] * s0_ref[...] + w[:, 1:2] * s1_ref[...])
    o_ref[...] = _rms(xn, g_ref[...]) if final_norm else xn


def _combine(x2, slots, top_w, g, final_norm):
    t = x2.shape[0]
    tm = min(512, t)
    nb = t // tm
    return pl.pallas_call(
        functools.partial(_combine_body, final_norm=final_norm),
        grid=(nb,),
        in_specs=[pl.BlockSpec((tm, D_MODEL), lambda i: (i, 0)),
                  pl.BlockSpec((tm, D_MODEL), lambda i: (i, 0)),
                  pl.BlockSpec((tm, D_MODEL), lambda i: (i + nb, 0)),
                  pl.BlockSpec((tm, TOP_K), lambda i: (i, 0)),
                  pl.BlockSpec((1, D_MODEL), lambda i: (0, 0))],
        out_specs=pl.BlockSpec((tm, D_MODEL), lambda i: (i, 0)),
        out_shape=jax.ShapeDtypeStruct((t, D_MODEL), F32),
        compiler_params=_params(1),
        name="moe_combine",
    )(x2, slots, slots, top_w, g)


def _final_norm_body(x_ref, g_ref, o_ref):
    o_ref[...] = _rms(x_ref[...], g_ref[...])


def _final_norm(x2, g):
    t = x2.shape[0]
    tm = min(512, t)
    return pl.pallas_call(
        _final_norm_body,
        grid=(t // tm,),
        in_specs=[pl.BlockSpec((tm, D_MODEL), lambda i: (i, 0)),
                  pl.BlockSpec((1, D_MODEL), lambda i: (0, 0))],
        out_specs=pl.BlockSpec((tm, D_MODEL), lambda i: (i, 0)),
        out_shape=jax.ShapeDtypeStruct((t, D_MODEL), F32),
        compiler_params=_params(1),
        name="final_norm",
    )(x2, g)


def _router_weights(w):
    wp = jnp.pad(w, ((0, 0), (0, ROUTER_PAD - N_EXPERTS)))
    hi = wp.astype(BF16)
    lo = (wp - hi.astype(F32)).astype(BF16)
    return jnp.concatenate([hi, lo], axis=1)


def kernel(x, g_mix, w_in, conv_w, w_branch_conv, w_branch_attn, w_out, g_ffn, w_ffn_gate, w_ffn_up, w_ffn_down, w_router, w_exp_gate, w_exp_up, w_exp_down, g_final):
    b, s, d = x.shape
    depth = g_mix.shape[0]
    t = b * s
    x2 = x.reshape(t, d)
    gf = g_final.reshape(1, d)
    for i in range(depth):
        last = i == depth - 1
        moe = i % 2 == 1
        j = i // 2
        qz, k, vz, ma, gbs = _mix_in(x2, g_mix[i].reshape(1, d), w_in[i].astype(BF16), conv_w[i],
                                     w_branch_conv[i].astype(BF16), s)
        ub = _attention(qz.reshape(b, s, D_SPLIT), k.reshape(b, s, D_ATTN),
                        vz.reshape(b, s, D_SPLIT)).reshape(t, D_ATTN)
        wb = w_branch_attn[i].astype(BF16)
        wo = w_out[i].astype(BF16)
        gi = g_ffn[i].reshape(1, d)
        if not moe:
            x2, h = _mix_out(x2, ma, gbs, ub, wb, wo, gi)
            x2 = _dense_ffn(x2, h, w_ffn_gate[j].astype(BF16), w_ffn_up[j].astype(BF16),
                            w_ffn_down[j].astype(BF16))
            if last:
                x2 = _final_norm(x2, gf)
        else:
            x2, h, logits = _mix_out(x2, ma, gbs, ub, wb, wo, gi, _router_weights(w_router[j]))
            tm = min(512, t)
            top_w, packed, tile_expert, n_valid, n_tiles = _route(logits[:, :N_EXPERTS], tm)
            slots = _experts(tile_expert, n_valid, packed, h,
                             w_exp_gate[j].astype(BF16), w_exp_up[j].astype(BF16),
                             w_exp_down[j].astype(BF16), tm, n_tiles)
            x2 = _combine(x2, slots, top_w, gf, final_norm=last)
    return x2.reshape(b, s, d)
```

```python
import functools
import math

import jax
import jax.numpy as jnp
from jax import lax
from jax.experimental import pallas as pl
from jax.experimental.pallas import tpu as pltpu

F32 = jnp.float32
BF16 = jnp.bfloat16

D_MODEL = 1024
D_CONV = 512
D_ATTN = 512
N_HEADS = 8
HEAD_DIM = 64
N_EXPERTS = 8
TOP_K = 2
EPS = 1e-6
CONV_WIDTH = 3

LANES = 128
VMEM_LIMIT_BYTES = 56 * 1024 * 1024

ATTN_BLOCK = 128
ATTN_GROUP = 4
N_PAIRS = N_HEADS // 2
D_SPLIT = 2 * D_ATTN
LOG2_F32_ZERO = -151.0
ROUTER_PAD = 128
ROW_CHUNKS = D_MODEL // LANES


def _params(n_grid, vmem=VMEM_LIMIT_BYTES):
    return pltpu.CompilerParams(
        dimension_semantics=("arbitrary",) * n_grid, vmem_limit_bytes=vmem)


def _rms(x, g):
    return x * lax.rsqrt(jnp.mean(x * x, axis=-1, keepdims=True) + EPS) * g


def _dot(a, b):
    return jnp.dot(a, b, preferred_element_type=F32)


def _neg_abs(x):
    bits = lax.bitcast_convert_type(x, jnp.uint32) | jnp.uint32(0x80000000)
    return lax.bitcast_convert_type(bits, F32)


def _const_spec(shape):
    return pl.BlockSpec(shape, lambda i: (0,) * len(shape), pipeline_mode=pl.Buffered(1))


def _mix_in_body(x_ref, g_ref, win_ref, cw_ref, wa_ref,
                 qz_ref, k_ref, vz_ref, ma_ref, gbs_ref, ubuf, *, tm, tiles_per_seq):
    i = pl.program_id(0)
    h = _rms(x_ref[...], g_ref[...]).astype(BF16)

    p = _dot(h, win_ref[:, 0:3 * D_CONV])
    u = p[:, 2 * D_CONV:3 * D_CONV] * p[:, 0:D_CONV]

    @pl.when(i % tiles_per_seq == 0)
    def _():
        ubuf[0:8, :] = jnp.zeros((8, D_CONV), F32)

    ubuf[8:8 + tm, :] = u
    cw = cw_ref[...]
    conv = (cw[0:1, :] * ubuf[6:6 + tm, :] + cw[1:2, :] * ubuf[7:7 + tm, :]) + cw[2:3, :] * u
    ubuf[0:8, :] = ubuf[tm:tm + 8, :]
    ua = (p[:, D_CONV:2 * D_CONV] * conv).astype(BF16)

    o = 3 * D_CONV
    qkv = _dot(h, win_ref[:, o:o + 3 * D_ATTN])
    first = lax.broadcasted_iota(jnp.int32, (tm, LANES), 1) < HEAD_DIM
    q_scale = (HEAD_DIM ** -0.5) * math.log2(math.e)
    for pr in range(N_PAIRS):
        qs = qkv[:, pr * LANES:(pr + 1) * LANES] * q_scale
        vs = qkv[:, 2 * D_ATTN + pr * LANES:2 * D_ATTN + (pr + 1) * LANES]
        lo, mid, hi = 2 * pr * LANES, (2 * pr + 1) * LANES, (2 * pr + 2) * LANES
        qz_ref[:, lo:mid] = jnp.where(first, qs, 0.0).astype(BF16)
        qz_ref[:, mid:hi] = jnp.where(first, 0.0, qs).astype(BF16)
        vz_ref[:, lo:mid] = jnp.where(first, vs, 0.0).astype(BF16)
        vz_ref[:, mid:hi] = jnp.where(first, 0.0, vs).astype(BF16)
    k_ref[...] = qkv[:, D_ATTN:2 * D_ATTN].astype(BF16)

    o += 3 * D_ATTN
    ga = _dot(h, win_ref[:, o:o + D_MODEL])
    ma_ref[...] = (jax.nn.sigmoid(ga) * _dot(ua, wa_ref[...])).astype(BF16)
    gb = _dot(h, win_ref[:, o + D_MODEL:o + 2 * D_MODEL])
    gbs_ref[...] = jax.nn.sigmoid(gb).astype(BF16)


def _mix_in(x2, g, win, cw, wa, seq):
    t = x2.shape[0]
    tm = min(512, seq)
    n_in = win.shape[1]
    body = functools.partial(_mix_in_body, tm=tm, tiles_per_seq=seq // tm)
    row = lambda w: pl.BlockSpec((tm, w), lambda i: (i, 0))
    return pl.pallas_call(
        body,
        grid=(t // tm,),
        in_specs=[row(D_MODEL), _const_spec((1, D_MODEL)), _const_spec((D_MODEL, n_in)),
                  _const_spec((CONV_WIDTH, D_CONV)), _const_spec((D_CONV, D_MODEL))],
        out_specs=[row(D_SPLIT), row(D_ATTN), row(D_SPLIT), row(D_MODEL), row(D_MODEL)],
        out_shape=[jax.ShapeDtypeStruct((t, D_SPLIT), BF16),
                   jax.ShapeDtypeStruct((t, D_ATTN), BF16),
                   jax.ShapeDtypeStruct((t, D_SPLIT), BF16),
                   jax.ShapeDtypeStruct((t, D_MODEL), BF16),
                   jax.ShapeDtypeStruct((t, D_MODEL), BF16)],
        scratch_shapes=[pltpu.VMEM((tm + 8, D_CONV), F32)],
        compiler_params=_params(1),
        name="mix_in",
    )(x2, g, win, cw, wa)


def _attn_body(qz_ref, k_ref, vz_ref, o_ref, r_scr, o_scr):
    i = pl.program_id(1)
    tb = ATTN_BLOCK
    row2 = lax.broadcasted_iota(jnp.int32, (2 * tb, tb), 0)
    col2 = lax.broadcasted_iota(jnp.int32, (2 * tb, tb), 1)
    causal2 = col2 < jnp.where(row2 >= tb, row2 - tb, row2)
    rj = lax.broadcasted_iota(jnp.int32, (tb, 2 * tb), 0)
    cs_ = lax.broadcasted_iota(jnp.int32, (tb, 2 * tb), 1)
    suffix = jnp.where(jnp.logical_or(rj > cs_, cs_ >= tb), 1.0, 0.0).astype(BF16)

    r_scr[...] = jnp.zeros_like(r_scr)
    o_scr[...] = jnp.zeros_like(o_scr)

    def visit(n, diag, groups):
        chains = [(g, p) for g in groups for p in range(N_PAIRS)]
        start = {g: pl.multiple_of((ATTN_GROUP * i + g - n) * tb, tb) for g in groups}
        zs = []
        for g, p in chains:
            rows = slice(g * tb, (g + 1) * tb)
            q2 = jnp.concatenate([qz_ref[0, rows, 2 * p * LANES:(2 * p + 1) * LANES],
                                  qz_ref[0, rows, (2 * p + 1) * LANES:(2 * p + 2) * LANES]], axis=0)
            kp = k_ref[0, pl.ds(start[g], tb), p * LANES:(p + 1) * LANES]
            zs.append(lax.dot_general(q2, kp, (((1,), (1,)), ((), ())),
                                      preferred_element_type=F32))
        log_betas, log_1ms = [], []
        for z in zs:
            sp = jnp.log2(1.0 + jnp.exp2(_neg_abs(z)))
            log_beta = jnp.minimum(z, 0.0) - sp
            log_1m = log_beta - z
            if diag:
                log_1m = jnp.where(causal2, log_1m, 0.0)
            log_betas.append(log_beta)
            log_1ms.append(log_1m.astype(BF16))
        c_all = _dot(jnp.concatenate(log_1ms, axis=0), suffix)
        rmax = None
        for ci, (g, p) in enumerate(chains):
            idx = g * N_PAIRS + p
            c = c_all[ci * 2 * tb:(ci + 1) * 2 * tb]
            r_old = r_scr[idx]
            a = jnp.exp2(log_betas[ci] + c[:, 0:tb] + r_old)
            if diag:
                a = jnp.where(causal2, a, 0.0)
            r_new = r_old + c[:, tb:2 * tb]
            r_scr[idx] = r_new
            rmax = r_new if rmax is None else jnp.maximum(rmax, r_new)
            a2 = jnp.concatenate([a[0:tb], a[tb:2 * tb]], axis=1).astype(BF16)
            v2 = jnp.concatenate(
                [vz_ref[0, pl.ds(start[g], tb), 2 * p * LANES:(2 * p + 1) * LANES],
                 vz_ref[0, pl.ds(start[g], tb), (2 * p + 1) * LANES:(2 * p + 2) * LANES]], axis=0)
            o_scr[idx] += _dot(a2, v2)
        return jnp.max(rmax, axis=0, keepdims=True)[0, 0]

    all_groups = list(range(ATTN_GROUP))
    m0 = visit(0, True, all_groups)

    def cond(c):
        n, m = c
        return jnp.logical_and(n <= ATTN_GROUP * i, m > LOG2_F32_ZERO)

    def body(c):
        n, _ = c
        return n + 1, visit(n, False, all_groups)

    n_end, m_end = lax.while_loop(cond, body, (jnp.int32(1), m0))

    for extra in range(1, ATTN_GROUP):
        @pl.when(jnp.logical_and(n_end == ATTN_GROUP * i + 1, m_end > LOG2_F32_ZERO))
        def _():
            visit(ATTN_GROUP * i + extra, False, all_groups[extra:])

    for g in range(ATTN_GROUP):
        for p in range(N_PAIRS):
            o_ref[0, g * tb:(g + 1) * tb, p * LANES:(p + 1) * LANES] = (
                o_scr[g * N_PAIRS + p].astype(o_ref.dtype))


def _attention(qz, k, vz):
    b, s, _ = k.shape
    tb = ATTN_BLOCK
    tq = ATTN_GROUP * tb
    n_chains = ATTN_GROUP * N_PAIRS
    return pl.pallas_call(
        _attn_body,
        grid=(b, s // tq),
        in_specs=[pl.BlockSpec((1, tq, D_SPLIT), lambda bi, i: (bi, i, 0)),
                  pl.BlockSpec((1, s, D_ATTN), lambda bi, i: (bi, 0, 0)),
                  pl.BlockSpec((1, s, D_SPLIT), lambda bi, i: (bi, 0, 0))],
        out_specs=pl.BlockSpec((1, tq, D_ATTN), lambda bi, i: (bi, i, 0)),
        out_shape=jax.ShapeDtypeStruct((b, s, D_ATTN), BF16),
        scratch_shapes=[pltpu.VMEM((n_chains, 2 * tb, tb), F32),
                        pltpu.VMEM((n_chains, tb, LANES), F32)],
        compiler_params=_params(2),
        name="stickbreak_attn",
    )(qz, k, vz)


def _merge(x_ref, ma_ref, gbs_ref, ub_ref, wb_ref, wo_ref, g_ref):
    mix = ma_ref[...].astype(F32) + gbs_ref[...].astype(F32) * _dot(ub_ref[...], wb_ref[...])
    xn = x_ref[...] + _dot(mix.astype(BF16), wo_ref[...])
    return xn, _rms(xn, g_ref[...])


def _mix_out_dense_body(x_ref, ma_ref, gbs_ref, ub_ref, wb_ref, wo_ref, g_ref,
                        wg_ref, wu_ref, wd_ref, o_ref):
    xn, h = _merge(x_ref, ma_ref, gbs_ref, ub_ref, wb_ref, wo_ref, g_ref)
    h = h.astype(BF16)
    act = (jax.nn.silu(_dot(h, wg_ref[...])) * _dot(h, wu_ref[...])).astype(BF16)
    o_ref[...] = xn + _dot(act, wd_ref[...])


def _mix_out_router_body(x_ref, ma_ref, gbs_ref, ub_ref, wb_ref, wo_ref, g_ref, wr_ref,
                         xo_ref, h_ref, lg_ref, *, tm):
    xn, h = _merge(x_ref, ma_ref, gbs_ref, ub_ref, wb_ref, wo_ref, g_ref)
    xo_ref[...] = xn
    for c in range(ROW_CHUNKS):
        h_ref[pl.ds(c, tm, stride=ROW_CHUNKS), :] = h[:, c * LANES:(c + 1) * LANES]
    h_hi = h.astype(BF16)
    h_lo = (h - h_hi.astype(F32)).astype(BF16)
    c2 = _dot(h_hi, wr_ref[...])
    lg_ref[...] = (c2[:, 0:ROUTER_PAD] + c2[:, ROUTER_PAD:2 * ROUTER_PAD]
                   + _dot(h_lo, wr_ref[:, 0:ROUTER_PAD]))


def _mix_out_specs(tm):
    row = lambda w: pl.BlockSpec((tm, w), lambda i: (i, 0))
    return [row(D_MODEL), row(D_MODEL), row(D_MODEL), row(D_ATTN),
            _const_spec((D_ATTN, D_MODEL)), _const_spec((D_MODEL, D_MODEL)),
            _const_spec((1, D_MODEL))]


def _mix_out_dense(x2, ma, gbs, ub, wb, wo, g, wg, wu, wd):
    t = x2.shape[0]
    tm = min(256, t)
    dff = wg.shape[1]
    return pl.pallas_call(
        _mix_out_dense_body,
        grid=(t // tm,),
        in_specs=_mix_out_specs(tm) + [_const_spec((D_MODEL, dff)), _const_spec((D_MODEL, dff)),
                                       _const_spec((dff, D_MODEL))],
        out_specs=pl.BlockSpec((tm, D_MODEL), lambda i: (i, 0)),
        out_shape=jax.ShapeDtypeStruct((t, D_MODEL), F32),
        compiler_params=_params(1),
        name="mix_out_dense_ffn",
    )(x2, ma, gbs, ub, wb, wo, g, wg, wu, wd)


def _mix_out_router(x2, ma, gbs, ub, wb, wo, g, wr):
    t = x2.shape[0]
    tm = min(512, t)
    row = lambda w: pl.BlockSpec((tm, w), lambda i: (i, 0))
    return pl.pallas_call(
        functools.partial(_mix_out_router_body, tm=tm),
        grid=(t // tm,),
        in_specs=_mix_out_specs(tm) + [_const_spec((D_MODEL, 2 * ROUTER_PAD))],
        out_specs=[row(D_MODEL), pl.BlockSpec((tm * ROW_CHUNKS, LANES), lambda i: (i, 0)),
                   row(ROUTER_PAD)],
        out_shape=[jax.ShapeDtypeStruct((t, D_MODEL), F32),
                   jax.ShapeDtypeStruct((t * ROW_CHUNKS, LANES), F32),
                   jax.ShapeDtypeStruct((t, ROUTER_PAD), F32)],
        compiler_params=_params(1),
        name="mix_out_router",
    )(x2, ma, gbs, ub, wb, wo, g, wr)


def _route(logits, tm):
    t = logits.shape[0]
    top_vals, top_idx = lax.top_k(logits, TOP_K)
    top_w = jax.nn.softmax(top_vals, axis=-1)
    e_flat = top_idx.T.reshape(-1).astype(jnp.int32)
    experts = jnp.arange(N_EXPERTS, dtype=jnp.int32)
    onehot = (e_flat[:, None] == experts[None, :]).astype(jnp.int32)
    csum = jnp.cumsum(onehot, axis=0)
    counts = csum[-1]
    starts = jnp.cumsum(counts) - counts
    pos = jnp.sum((csum - 1 + starts[None, :]) * onehot, axis=1).astype(jnp.int32)
    order = _invert_permutation(pos)
    tiles_e = (counts + tm - 1) // tm
    tile_end = jnp.cumsum(tiles_e)
    tile_begin = tile_end - tiles_e
    n_tiles = (TOP_K * t) // tm + N_EXPERTS
    ti = jnp.arange(n_tiles, dtype=jnp.int32)
    tile_expert = jnp.minimum(
        jnp.sum((ti[:, None] >= tile_end[None, :]).astype(jnp.int32), axis=1), N_EXPERTS - 1)
    local = ti - tile_begin[tile_expert]
    row0 = starts[tile_expert] + local * tm
    n_rows = jnp.clip(counts[tile_expert] - local * tm, 0, tm)
    r = jnp.arange(tm, dtype=jnp.int32)[None, :]
    real = r < n_rows[:, None]
    a = order[jnp.minimum(row0[:, None] + r, TOP_K * t - 1)]
    tok = jnp.where(real, jnp.where(a >= t, a - t, a), 0)
    dst = jnp.where(real, a, TOP_K * t + (ti[:, None] % 2) * tm + r)
    tok_bits = _token_bits(t)
    packed = lax.bitcast_convert_type(
        tok.astype(jnp.uint32) | (dst.astype(jnp.uint32) << tok_bits), jnp.int32).reshape(-1)
    n_valid = tile_end[-1].astype(jnp.int32).reshape(1)
    return top_w, packed, tile_expert.astype(jnp.int32), n_valid, n_tiles


def _token_bits(t):
    bits = max(1, (t - 1).bit_length())
    assert bits + (TOP_K * t + 2 * t - 1).bit_length() <= 32, "row index does not pack in 32 bits"
    return bits


def _invert_body(pos_ref, order_ref, *, n):
    def body(a, c):
        order_ref[pos_ref[a]] = a
        return c
    lax.fori_loop(0, n, body, 0, unroll=16)


def _invert_permutation(pos):
    n = pos.shape[0]
    return pl.pallas_call(
        functools.partial(_invert_body, n=n),
        in_specs=[pl.BlockSpec(memory_space=pltpu.SMEM)],
        out_specs=pl.BlockSpec(memory_space=pltpu.SMEM),
        out_shape=jax.ShapeDtypeStruct((n,), jnp.int32),
        name="moe_invert",
    )(pos)


def _expert_body(te_ref, nv_ref, idx_ref, h_ref, wg_ref, wu_ref, wd_ref,
                 slots_ref, xbuf, ybuf, gsem, ssem, *, tm, t):
    del te_ref
    i = pl.program_id(0)
    n_valid = nv_ref[0]
    slot = i % 2
    tok_bits = _token_bits(t)
    tok_mask = (1 << tok_bits) - 1
    rc = ROW_CHUNKS

    def tile_rows(r):
        return pl.ds(pl.multiple_of(r * rc, rc), rc)

    def row_loop(tile, issue):
        base = tile * tm

        def body(g, c):
            r8 = g * 8
            for j in range(8):
                issue(idx_ref[base + r8 + j], r8 + j)
            return c
        lax.fori_loop(0, tm // 8, body, 0)

    def gather_start(tile, sl):
        def issue(packed, r):
            tok = jnp.bitwise_and(packed, tok_mask)
            pltpu.make_async_copy(h_ref.at[tile_rows(tok)], xbuf.at[sl, tile_rows(r)],
                                  gsem.at[sl]).start()
        row_loop(tile, issue)

    def gather_wait(sl):
        pltpu.make_async_copy(h_ref.at[pl.ds(0, tm * rc)], xbuf.at[sl], gsem.at[sl]).wait()

    def scatter_start(tile, sl):
        def issue(packed, r):
            dst = lax.shift_right_logical(packed, tok_bits)
            pltpu.make_async_copy(ybuf.at[sl, tile_rows(r)], slots_ref.at[tile_rows(dst)],
                                  ssem.at[sl]).start()
        row_loop(tile, issue)

    def scatter_wait(sl):
        pltpu.make_async_copy(ybuf.at[sl], slots_ref.at[pl.ds(0, tm * rc)], ssem.at[sl]).wait()

    @pl.when(i == 0)
    def _():
        ybuf[0] = jnp.zeros((tm * rc, LANES), F32)
        for sl in range(2):
            dump = pltpu.make_async_copy(
                ybuf.at[0], slots_ref.at[pl.ds((TOP_K * t + sl * tm) * rc, tm * rc)], ssem.at[0])
            dump.start()
            dump.wait()

    @pl.when(jnp.logical_and(i == 0, n_valid > 0))
    def _():
        gather_start(0, 0)

    @pl.when(i + 1 < n_valid)
    def _():
        gather_start(i + 1, 1 - slot)

    @pl.when(i < n_valid)
    def _():
        gather_wait(slot)

        @pl.when(i >= 2)
        def _():
            scatter_wait(slot)

        x = jnp.concatenate(
            [xbuf[slot, pl.ds(c, tm, stride=rc), :].astype(BF16) for c in range(rc)], axis=1)
        act = (jax.nn.silu(_dot(x, wg_ref[0])) * _dot(x, wu_ref[0])).astype(BF16)
        y = _dot(act, wd_ref[0])
        for c in range(rc):
            ybuf[slot, pl.ds(c, tm, stride=rc), :] = y[:, c * LANES:(c + 1) * LANES]
        scatter_start(i, slot)

        @pl.when(i == n_valid - 1)
        def _():
            @pl.when(i >= 1)
            def _():
                scatter_wait(1 - slot)
            scatter_wait(slot)


def _experts(tile_expert, n_valid, packed, h, wg, wu, wd, tm, n_tiles):
    t = h.shape[0] // ROW_CHUNKS
    dfe = wg.shape[2]
    wmap = lambda i, te, nv, ix: (te[i], 0, 0)
    return pl.pallas_call(
        functools.partial(_expert_body, tm=tm, t=t),
        grid_spec=pltpu.PrefetchScalarGridSpec(
            num_scalar_prefetch=3,
            grid=(n_tiles,),
            in_specs=[pl.BlockSpec(memory_space=pl.ANY),
                      pl.BlockSpec((1, D_MODEL, dfe), wmap),
                      pl.BlockSpec((1, D_MODEL, dfe), wmap),
                      pl.BlockSpec((1, dfe, D_MODEL), wmap)],
            out_specs=pl.BlockSpec(memory_space=pl.ANY),
            scratch_shapes=[pltpu.VMEM((2, tm * ROW_CHUNKS, LANES), F32),
                            pltpu.VMEM((2, tm * ROW_CHUNKS, LANES), F32),
                            pltpu.SemaphoreType.DMA((2,)),
                            pltpu.SemaphoreType.DMA((2,))],
        ),
        out_shape=jax.ShapeDtypeStruct(((TOP_K * t + 2 * tm) * ROW_CHUNKS, LANES), F32),
        compiler_params=_params(1),
        name="moe_experts",
    )(tile_expert, n_valid, packed, h, wg, wu, wd)


def _combine_body(x_ref, s0_ref, s1_ref, w_ref, g_ref, o_ref, *, tm, final_norm):
    w = w_ref[...]
    rows = lambda ref: jnp.concatenate(
        [ref[pl.ds(c, tm, stride=ROW_CHUNKS), :] for c in range(ROW_CHUNKS)], axis=1)
    xn = x_ref[...] + (w[:, 0:1] * rows(s0_ref) + w[:, 1:2] * rows(s1_ref))
    o_ref[...] = _rms(xn, g_ref[...]) if final_norm else xn


def _combine(x2, slots, top_w, g, final_norm):
    t = x2.shape[0]
    tm = min(512, t)
    nb = t // tm
    return pl.pallas_call(
        functools.partial(_combine_body, tm=tm, final_norm=final_norm),
        grid=(nb,),
        in_specs=[pl.BlockSpec((tm, D_MODEL), lambda i: (i, 0)),
                  pl.BlockSpec((tm * ROW_CHUNKS, LANES), lambda i: (i, 0)),
                  pl.BlockSpec((tm * ROW_CHUNKS, LANES), lambda i: (i + nb, 0)),
                  pl.BlockSpec((tm, TOP_K), lambda i: (i, 0)),
                  pl.BlockSpec((1, D_MODEL), lambda i: (0, 0))],
        out_specs=pl.BlockSpec((tm, D_MODEL), lambda i: (i, 0)),
        out_shape=jax.ShapeDtypeStruct((t, D_MODEL), F32),
        compiler_params=_params(1),
        name="moe_combine",
    )(x2, slots, slots, top_w, g)


def _final_norm_body(x_ref, g_ref, o_ref):
    o_ref[...] = _rms(x_ref[...], g_ref[...])


def _final_norm(x2, g):
    t = x2.shape[0]
    tm = min(512, t)
    return pl.pallas_call(
        _final_norm_body,
        grid=(t // tm,),
        in_specs=[pl.BlockSpec((tm, D_MODEL), lambda i: (i, 0)),
                  pl.BlockSpec((1, D_MODEL), lambda i: (0, 0))],
        out_specs=pl.BlockSpec((tm, D_MODEL), lambda i: (i, 0)),
        out_shape=jax.ShapeDtypeStruct((t, D_MODEL), F32),
        compiler_params=_params(1),
        name="final_norm",
    )(x2, g)


def _router_weights(w):
    wp = jnp.pad(w, ((0, 0), (0, ROUTER_PAD - N_EXPERTS)))
    hi = wp.astype(BF16)
    lo = (wp - hi.astype(F32)).astype(BF16)
    return jnp.concatenate([hi, lo], axis=1)


def kernel(x, g_mix, w_in, conv_w, w_branch_conv, w_branch_attn, w_out, g_ffn, w_ffn_gate, w_ffn_up, w_ffn_down, w_router, w_exp_gate, w_exp_up, w_exp_down, g_final):
    b, s, d = x.shape
    depth = g_mix.shape[0]
    t = b * s
    x2 = x.reshape(t, d)
    gf = g_final.reshape(1, d)
    for i in range(depth):
        last = i == depth - 1
        moe = i % 2 == 1
        j = i // 2
        qz, k, vz, ma, gbs = _mix_in(x2, g_mix[i].reshape(1, d), w_in[i].astype(BF16), conv_w[i],
                                     w_branch_conv[i].astype(BF16), s)
        ub = _attention(qz.reshape(b, s, D_SPLIT), k.reshape(b, s, D_ATTN),
                        vz.reshape(b, s, D_SPLIT)).reshape(t, D_ATTN)
        wb = w_branch_attn[i].astype(BF16)
        wo = w_out[i].astype(BF16)
        gi = g_ffn[i].reshape(1, d)
        if not moe:
            x2 = _mix_out_dense(x2, ma, gbs, ub, wb, wo, gi, w_ffn_gate[j].astype(BF16),
                                w_ffn_up[j].astype(BF16), w_ffn_down[j].astype(BF16))
            if last:
                x2 = _final_norm(x2, gf)
        else:
            x2, h, logits = _mix_out_router(x2, ma, gbs, ub, wb, wo, gi,
                                            _router_weights(w_router[j]))
            tm = min(512, t)
            top_w, packed, tile_expert, n_valid, n_tiles = _route(logits[:, :N_EXPERTS], tm)
            slots = _experts(tile_expert, n_valid, packed, h,
                             w_exp_gate[j].astype(BF16), w_exp_up[j].astype(BF16),
                             w_exp_down[j].astype(BF16), tm, n_tiles)
            x2 = _combine(x2, slots, top_w, gf, final_norm=last)
    return x2.reshape(b, s, d)
```

```python
import functools
import math

import jax
import jax.numpy as jnp
from jax import lax
from jax.experimental import pallas as pl
from jax.experimental.pallas import tpu as pltpu

F32 = jnp.float32
BF16 = jnp.bfloat16

D_MODEL = 1024
D_CONV = 512
D_ATTN = 512
N_HEADS = 8
HEAD_DIM = 64
N_EXPERTS = 8
TOP_K = 2
EPS = 1e-6
CONV_WIDTH = 3

LANES = 128
VMEM_LIMIT_BYTES = 56 * 1024 * 1024

ATTN_BLOCK = 128
ATTN_GROUP = 4
N_PAIRS = N_HEADS // 2
D_SPLIT = 2 * D_ATTN
LOG2_F32_ZERO = -151.0
ROUTER_PAD = 128
ROW_CHUNKS = D_MODEL // LANES


def _params(n_grid, vmem=VMEM_LIMIT_BYTES):
    return pltpu.CompilerParams(
        dimension_semantics=("arbitrary",) * n_grid, vmem_limit_bytes=vmem)


def _rms(x, g):
    return x * lax.rsqrt(jnp.mean(x * x, axis=-1, keepdims=True) + EPS) * g


def _dot(a, b):
    return jnp.dot(a, b, preferred_element_type=F32)


def _const_spec(shape):
    return pl.BlockSpec(shape, lambda i: (0,) * len(shape), pipeline_mode=pl.Buffered(1))


def _mix_in_body(x_ref, g_ref, win_ref, cw_ref, wa_ref,
                 qz_ref, k_ref, vz_ref, ma_ref, gbs_ref, ubuf, *, tm, tiles_per_seq):
    i = pl.program_id(0)
    h = _rms(x_ref[...], g_ref[...]).astype(BF16)

    p = _dot(h, win_ref[:, 0:3 * D_CONV])
    u = p[:, 2 * D_CONV:3 * D_CONV] * p[:, 0:D_CONV]

    @pl.when(i % tiles_per_seq == 0)
    def _():
        ubuf[0:8, :] = jnp.zeros((8, D_CONV), F32)

    ubuf[8:8 + tm, :] = u
    cw = cw_ref[...]
    conv = (cw[0:1, :] * ubuf[6:6 + tm, :] + cw[1:2, :] * ubuf[7:7 + tm, :]) + cw[2:3, :] * u
    ubuf[0:8, :] = ubuf[tm:tm + 8, :]
    ua = (p[:, D_CONV:2 * D_CONV] * conv).astype(BF16)

    o = 3 * D_CONV
    qkv = _dot(h, win_ref[:, o:o + 3 * D_ATTN])
    o += 3 * D_ATTN
    ga = _dot(h, win_ref[:, o:o + D_MODEL])
    gb = _dot(h, win_ref[:, o + D_MODEL:o + 2 * D_MODEL])

    first = lax.broadcasted_iota(jnp.int32, (tm, LANES), 1) < HEAD_DIM
    q_scale = (HEAD_DIM ** -0.5) * math.log2(math.e)
    for pr in range(N_PAIRS):
        qs = qkv[:, pr * LANES:(pr + 1) * LANES] * q_scale
        vs = qkv[:, 2 * D_ATTN + pr * LANES:2 * D_ATTN + (pr + 1) * LANES]
        lo, mid, hi = 2 * pr * LANES, (2 * pr + 1) * LANES, (2 * pr + 2) * LANES
        qz_ref[:, lo:mid] = jnp.where(first, qs, 0.0).astype(BF16)
        qz_ref[:, mid:hi] = jnp.where(first, 0.0, qs).astype(BF16)
        vz_ref[:, lo:mid] = jnp.where(first, vs, 0.0).astype(BF16)
        vz_ref[:, mid:hi] = jnp.where(first, 0.0, vs).astype(BF16)
    k_ref[...] = qkv[:, D_ATTN:2 * D_ATTN].astype(BF16)
    gbs_ref[...] = jax.nn.sigmoid(gb).astype(BF16)
    ma_ref[...] = (jax.nn.sigmoid(ga) * _dot(ua, wa_ref[...])).astype(BF16)


def _mix_in(x2, g, win, cw, wa, seq):
    t = x2.shape[0]
    tm = min(512, seq)
    n_in = win.shape[1]
    body = functools.partial(_mix_in_body, tm=tm, tiles_per_seq=seq // tm)
    row = lambda w: pl.BlockSpec((tm, w), lambda i: (i, 0))
    return pl.pallas_call(
        body,
        grid=(t // tm,),
        in_specs=[row(D_MODEL), _const_spec((1, D_MODEL)), _const_spec((D_MODEL, n_in)),
                  _const_spec((CONV_WIDTH, D_CONV)), _const_spec((D_CONV, D_MODEL))],
        out_specs=[row(D_SPLIT), row(D_ATTN), row(D_SPLIT), row(D_MODEL), row(D_MODEL)],
        out_shape=[jax.ShapeDtypeStruct((t, D_SPLIT), BF16),
                   jax.ShapeDtypeStruct((t, D_ATTN), BF16),
                   jax.ShapeDtypeStruct((t, D_SPLIT), BF16),
                   jax.ShapeDtypeStruct((t, D_MODEL), BF16),
                   jax.ShapeDtypeStruct((t, D_MODEL), BF16)],
        scratch_shapes=[pltpu.VMEM((tm + 8, D_CONV), F32)],
        compiler_params=_params(1),
        name="mix_in",
    )(x2, g, win, cw, wa)


def _attn_body(qz_ref, k_ref, vz_ref, o_ref, r_scr, o_scr):
    i = pl.program_id(1)
    tb = ATTN_BLOCK
    row2 = lax.broadcasted_iota(jnp.int32, (2 * tb, tb), 0)
    col2 = lax.broadcasted_iota(jnp.int32, (2 * tb, tb), 1)
    causal2 = col2 < jnp.where(row2 >= tb, row2 - tb, row2)
    rj = lax.broadcasted_iota(jnp.int32, (tb, 2 * tb), 0)
    cs_ = lax.broadcasted_iota(jnp.int32, (tb, 2 * tb), 1)
    suffix = jnp.where(jnp.logical_or(rj > cs_, cs_ >= tb), 1.0, 0.0).astype(BF16)

    r_scr[...] = jnp.zeros_like(r_scr)
    o_scr[...] = jnp.zeros_like(o_scr)

    def visit(n, diag, groups):
        chains = [(g, p) for g in groups for p in range(N_PAIRS)]
        start = {g: pl.multiple_of((ATTN_GROUP * i + g - n) * tb, tb) for g in groups}
        zs = []
        for g, p in chains:
            rows = slice(g * tb, (g + 1) * tb)
            q2 = jnp.concatenate([qz_ref[0, rows, 2 * p * LANES:(2 * p + 1) * LANES],
                                  qz_ref[0, rows, (2 * p + 1) * LANES:(2 * p + 2) * LANES]], axis=0)
            kp = k_ref[0, pl.ds(start[g], tb), p * LANES:(p + 1) * LANES]
            zs.append(lax.dot_general(q2, kp, (((1,), (1,)), ((), ())),
                                      preferred_element_type=F32))
        log_betas, log_1ms = [], []
        for z in zs:
            sp = jnp.log2(1.0 + jnp.exp2(-jnp.abs(z)))
            log_beta = jnp.minimum(z, 0.0) - sp
            log_1m = log_beta - z
            if diag:
                log_1m = jnp.where(causal2, log_1m, 0.0)
            log_betas.append(log_beta)
            log_1ms.append(log_1m.astype(BF16))
        c_all = _dot(jnp.concatenate(log_1ms, axis=0), suffix)
        rmax = None
        for ci, (g, p) in enumerate(chains):
            idx = g * N_PAIRS + p
            c = c_all[ci * 2 * tb:(ci + 1) * 2 * tb]
            r_old = r_scr[idx]
            a = jnp.exp2(log_betas[ci] + c[:, 0:tb] + r_old)
            if diag:
                a = jnp.where(causal2, a, 0.0)
            r_new = r_old + c[:, tb:2 * tb]
            r_scr[idx] = r_new
            rmax = r_new if rmax is None else jnp.maximum(rmax, r_new)
            a2 = jnp.concatenate([a[0:tb], a[tb:2 * tb]], axis=1).astype(BF16)
            v2 = jnp.concatenate(
                [vz_ref[0, pl.ds(start[g], tb), 2 * p * LANES:(2 * p + 1) * LANES],
                 vz_ref[0, pl.ds(start[g], tb), (2 * p + 1) * LANES:(2 * p + 2) * LANES]], axis=0)
            o_scr[idx] += _dot(a2, v2)
        return jnp.max(rmax, axis=0, keepdims=True)[0, 0]

    all_groups = list(range(ATTN_GROUP))
    m0 = visit(0, True, all_groups)

    def cond(c):
        n, m = c
        return jnp.logical_and(n <= ATTN_GROUP * i, m > LOG2_F32_ZERO)

    def body(c):
        n, _ = c
        return n + 1, visit(n, False, all_groups)

    n_end, m_end = lax.while_loop(cond, body, (jnp.int32(1), m0))

    for extra in range(1, ATTN_GROUP):
        @pl.when(jnp.logical_and(n_end == ATTN_GROUP * i + 1, m_end > LOG2_F32_ZERO))
        def _():
            visit(ATTN_GROUP * i + extra, False, all_groups[extra:])

    for g in range(ATTN_GROUP):
        for p in range(N_PAIRS):
            o_ref[0, g * tb:(g + 1) * tb, p * LANES:(p + 1) * LANES] = (
                o_scr[g * N_PAIRS + p].astype(o_ref.dtype))


def _attention(qz, k, vz):
    b, s, _ = k.shape
    tb = ATTN_BLOCK
    tq = ATTN_GROUP * tb
    n_chains = ATTN_GROUP * N_PAIRS
    return pl.pallas_call(
        _attn_body,
        grid=(b, s // tq),
        in_specs=[pl.BlockSpec((1, tq, D_SPLIT), lambda bi, i: (bi, i, 0)),
                  pl.BlockSpec((1, s, D_ATTN), lambda bi, i: (bi, 0, 0)),
                  pl.BlockSpec((1, s, D_SPLIT), lambda bi, i: (bi, 0, 0))],
        out_specs=pl.BlockSpec((1, tq, D_ATTN), lambda bi, i: (bi, i, 0)),
        out_shape=jax.ShapeDtypeStruct((b, s, D_ATTN), BF16),
        scratch_shapes=[pltpu.VMEM((n_chains, 2 * tb, tb), F32),
                        pltpu.VMEM((n_chains, tb, LANES), F32)],
        compiler_params=_params(2),
        name="stickbreak_attn",
    )(qz, k, vz)


def _merge(x_ref, ma_ref, gbs_ref, ub_ref, wb_ref, wo_ref, g_ref):
    mix = ma_ref[...].astype(F32) + gbs_ref[...].astype(F32) * _dot(ub_ref[...], wb_ref[...])
    xn = x_ref[...] + _dot(mix.astype(BF16), wo_ref[...])
    return xn, _rms(xn, g_ref[...])


def _mix_out_dense_body(x_ref, ma_ref, gbs_ref, ub_ref, wb_ref, wo_ref, g_ref,
                        wg_ref, wu_ref, wd_ref, o_ref):
    xn, h = _merge(x_ref, ma_ref, gbs_ref, ub_ref, wb_ref, wo_ref, g_ref)
    h = h.astype(BF16)
    act = (jax.nn.silu(_dot(h, wg_ref[...])) * _dot(h, wu_ref[...])).astype(BF16)
    o_ref[...] = xn + _dot(act, wd_ref[...])


def _mix_out_router_body(x_ref, ma_ref, gbs_ref, ub_ref, wb_ref, wo_ref, g_ref, wr_ref,
                         xo_ref, h_ref, lg_ref, *, tm):
    xn, h = _merge(x_ref, ma_ref, gbs_ref, ub_ref, wb_ref, wo_ref, g_ref)
    xo_ref[...] = xn
    for c in range(ROW_CHUNKS):
        h_ref[pl.ds(c, tm, stride=ROW_CHUNKS), :] = h[:, c * LANES:(c + 1) * LANES]
    h_hi = h.astype(BF16)
    h_lo = (h - h_hi.astype(F32)).astype(BF16)
    c2 = _dot(h_hi, wr_ref[...])
    lg_ref[...] = (c2[:, 0:ROUTER_PAD] + c2[:, ROUTER_PAD:2 * ROUTER_PAD]
                   + _dot(h_lo, wr_ref[:, 0:ROUTER_PAD]))


def _mix_out_specs(tm):
    row = lambda w: pl.BlockSpec((tm, w), lambda i: (i, 0))
    return [row(D_MODEL), row(D_MODEL), row(D_MODEL), row(D_ATTN),
            _const_spec((D_ATTN, D_MODEL)), _const_spec((D_MODEL, D_MODEL)),
            _const_spec((1, D_MODEL))]


def _mix_out_dense(x2, ma, gbs, ub, wb, wo, g, wg, wu, wd):
    t = x2.shape[0]
    tm = min(256, t)
    dff = wg.shape[1]
    return pl.pallas_call(
        _mix_out_dense_body,
        grid=(t // tm,),
        in_specs=_mix_out_specs(tm) + [_const_spec((D_MODEL, dff)), _const_spec((D_MODEL, dff)),
                                       _const_spec((dff, D_MODEL))],
        out_specs=pl.BlockSpec((tm, D_MODEL), lambda i: (i, 0)),
        out_shape=jax.ShapeDtypeStruct((t, D_MODEL), F32),
        compiler_params=_params(1),
        name="mix_out_dense_ffn",
    )(x2, ma, gbs, ub, wb, wo, g, wg, wu, wd)


def _mix_out_router(x2, ma, gbs, ub, wb, wo, g, wr):
    t = x2.shape[0]
    tm = min(512, t)
    row = lambda w: pl.BlockSpec((tm, w), lambda i: (i, 0))
    return pl.pallas_call(
        functools.partial(_mix_out_router_body, tm=tm),
        grid=(t // tm,),
        in_specs=_mix_out_specs(tm) + [_const_spec((D_MODEL, 2 * ROUTER_PAD))],
        out_specs=[row(D_MODEL), pl.BlockSpec((tm * ROW_CHUNKS, LANES), lambda i: (i, 0)),
                   row(ROUTER_PAD)],
        out_shape=[jax.ShapeDtypeStruct((t, D_MODEL), F32),
                   jax.ShapeDtypeStruct((t * ROW_CHUNKS, LANES), F32),
                   jax.ShapeDtypeStruct((t, ROUTER_PAD), F32)],
        compiler_params=_params(1),
        name="mix_out_router",
    )(x2, ma, gbs, ub, wb, wo, g, wr)


def _route(logits, tm):
    t = logits.shape[0]
    top_vals, top_idx = lax.top_k(logits, TOP_K)
    top_w = jax.nn.softmax(top_vals, axis=-1)
    e_flat = top_idx.T.reshape(-1).astype(jnp.int32)
    experts = jnp.arange(N_EXPERTS, dtype=jnp.int32)
    onehot = (e_flat[:, None] == experts[None, :]).astype(jnp.int32)
    csum = jnp.cumsum(onehot, axis=0)
    counts = csum[-1]
    starts = jnp.cumsum(counts) - counts
    pos = jnp.sum((csum - 1 + starts[None, :]) * onehot, axis=1).astype(jnp.int32)
    order = _invert_permutation(pos)
    tiles_e = (counts + tm - 1) // tm
    tile_end = jnp.cumsum(tiles_e)
    tile_begin = tile_end - tiles_e
    n_tiles = (TOP_K * t) // tm + N_EXPERTS
    ti = jnp.arange(n_tiles, dtype=jnp.int32)
    tile_expert = jnp.minimum(
        jnp.sum((ti[:, None] >= tile_end[None, :]).astype(jnp.int32), axis=1), N_EXPERTS - 1)
    local = ti - tile_begin[tile_expert]
    row0 = starts[tile_expert] + local * tm
    n_rows = jnp.clip(counts[tile_expert] - local * tm, 0, tm)
    r = jnp.arange(tm, dtype=jnp.int32)[None, :]
    a = order[jnp.minimum(row0[:, None] + r, TOP_K * t - 1)]
    row_token = jnp.where(r < n_rows[:, None], jnp.where(a >= t, a - t, a), 0).reshape(-1)
    row_of = jnp.sum((csum - 1 + (tile_begin * tm)[None, :]) * onehot, axis=1).astype(jnp.int32)
    n_valid = tile_end[-1].astype(jnp.int32).reshape(1)
    return top_w, row_token.astype(jnp.int32), row_of, tile_expert.astype(jnp.int32), n_valid, n_tiles


def _invert_body(pos_ref, order_ref, *, n):
    def body(a, c):
        order_ref[pos_ref[a]] = a
        return c
    lax.fori_loop(0, n, body, 0, unroll=16)


def _invert_permutation(pos):
    n = pos.shape[0]
    return pl.pallas_call(
        functools.partial(_invert_body, n=n),
        in_specs=[pl.BlockSpec(memory_space=pltpu.SMEM)],
        out_specs=pl.BlockSpec(memory_space=pltpu.SMEM),
        out_shape=jax.ShapeDtypeStruct((n,), jnp.int32),
        name="moe_invert",
    )(pos)


def _row_tile(r):
    return pl.ds(pl.multiple_of(r * ROW_CHUNKS, ROW_CHUNKS), ROW_CHUNKS)


def _row_dma_loop(n_rows, issue):
    def body(g, c):
        for j in range(8):
            issue(g * 8 + j)
        return c
    lax.fori_loop(0, n_rows // 8, body, 0)


def _expert_body(te_ref, nv_ref, tok_ref, h_ref, wg_ref, wu_ref, wd_ref, y_ref, xbuf, gsem,
                 *, tm):
    del te_ref
    i = pl.program_id(0)
    n_valid = nv_ref[0]
    slot = i % 2
    rc = ROW_CHUNKS

    def gather_start(tile, sl):
        def issue(r):
            pltpu.make_async_copy(h_ref.at[_row_tile(tok_ref[tile * tm + r])],
                                  xbuf.at[sl, _row_tile(r)], gsem.at[sl]).start()
        _row_dma_loop(tm, issue)

    @pl.when(jnp.logical_and(i == 0, n_valid > 0))
    def _():
        gather_start(0, 0)

    @pl.when(i + 1 < n_valid)
    def _():
        gather_start(i + 1, 1 - slot)

    @pl.when(i < n_valid)
    def _():
        pltpu.make_async_copy(h_ref.at[pl.ds(0, tm * rc)], xbuf.at[slot], gsem.at[slot]).wait()
        x = jnp.concatenate(
            [xbuf[slot, pl.ds(c, tm, stride=rc), :].astype(BF16) for c in range(rc)], axis=1)
        act = (jax.nn.silu(_dot(x, wg_ref[0])) * _dot(x, wu_ref[0])).astype(BF16)
        y = _dot(act, wd_ref[0])
        for c in range(rc):
            y_ref[pl.ds(c, tm, stride=rc), :] = y[:, c * LANES:(c + 1) * LANES]

    @pl.when(i >= n_valid)
    def _():
        y_ref[...] = jnp.zeros_like(y_ref)


def _experts(tile_expert, n_valid, row_token, h, wg, wu, wd, tm, n_tiles):
    dfe = wg.shape[2]
    wmap = lambda i, te, nv, tk: (te[i], 0, 0)
    return pl.pallas_call(
        functools.partial(_expert_body, tm=tm),
        grid_spec=pltpu.PrefetchScalarGridSpec(
            num_scalar_prefetch=3,
            grid=(n_tiles,),
            in_specs=[pl.BlockSpec(memory_space=pl.ANY),
                      pl.BlockSpec((1, D_MODEL, dfe), wmap),
                      pl.BlockSpec((1, D_MODEL, dfe), wmap),
                      pl.BlockSpec((1, dfe, D_MODEL), wmap)],
            out_specs=pl.BlockSpec((tm * ROW_CHUNKS, LANES), lambda i, te, nv, tk: (i, 0)),
            scratch_shapes=[pltpu.VMEM((2, tm * ROW_CHUNKS, LANES), F32),
                            pltpu.SemaphoreType.DMA((2,))],
        ),
        out_shape=jax.ShapeDtypeStruct((n_tiles * tm * ROW_CHUNKS, LANES), F32),
        compiler_params=_params(1),
        name="moe_experts",
    )(tile_expert, n_valid, row_token, h, wg, wu, wd)


def _combine_body(row_ref, x_ref, w_ref, g_ref, y_ref, o_ref, ybuf, sem,
                  *, tm, t, n_steps, final_norm):
    i = pl.program_id(0)
    slot = i % 2
    rc = ROW_CHUNKS

    def gather_start(step, sl):
        def issue(r):
            for k in range(TOP_K):
                pltpu.make_async_copy(y_ref.at[_row_tile(row_ref[k * t + step * tm + r])],
                                      ybuf.at[sl, _row_tile(k * tm + r)], sem.at[sl]).start()
        _row_dma_loop(tm, issue)

    @pl.when(i == 0)
    def _():
        gather_start(0, 0)

    @pl.when(i + 1 < n_steps)
    def _():
        gather_start(i + 1, 1 - slot)

    pltpu.make_async_copy(y_ref.at[pl.ds(0, TOP_K * tm * rc)], ybuf.at[slot], sem.at[slot]).wait()
    rows = lambda k: jnp.concatenate(
        [ybuf[slot, pl.ds(k * tm * rc + c, tm, stride=rc), :] for c in range(rc)], axis=1)
    w = w_ref[...]
    xn = x_ref[...] + (w[:, 0:1] * rows(0) + w[:, 1:2] * rows(1))
    o_ref[...] = _rms(xn, g_ref[...]) if final_norm else xn


def _combine(row_of, x2, y, top_w, g, final_norm):
    t = x2.shape[0]
    tm = min(512, t)
    n_steps = t // tm
    return pl.pallas_call(
        functools.partial(_combine_body, tm=tm, t=t, n_steps=n_steps, final_norm=final_norm),
        grid_spec=pltpu.PrefetchScalarGridSpec(
            num_scalar_prefetch=1,
            grid=(n_steps,),
            in_specs=[pl.BlockSpec((tm, D_MODEL), lambda i, ro: (i, 0)),
                      pl.BlockSpec((tm, TOP_K), lambda i, ro: (i, 0)),
                      pl.BlockSpec((1, D_MODEL), lambda i, ro: (0, 0)),
                      pl.BlockSpec(memory_space=pl.ANY)],
            out_specs=pl.BlockSpec((tm, D_MODEL), lambda i, ro: (i, 0)),
            scratch_shapes=[pltpu.VMEM((2, TOP_K * tm * ROW_CHUNKS, LANES), F32),
                            pltpu.SemaphoreType.DMA((2,))],
        ),
        out_shape=jax.ShapeDtypeStruct((t, D_MODEL), F32),
        compiler_params=_params(1),
        name="moe_combine",
    )(row_of, x2, top_w, g, y)


def _final_norm_body(x_ref, g_ref, o_ref):
    o_ref[...] = _rms(x_ref[...], g_ref[...])


def _final_norm(x2, g):
    t = x2.shape[0]
    tm = min(512, t)
    return pl.pallas_call(
        _final_norm_body,
        grid=(t // tm,),
        in_specs=[pl.BlockSpec((tm, D_MODEL), lambda i: (i, 0)),
                  pl.BlockSpec((1, D_MODEL), lambda i: (0, 0))],
        out_specs=pl.BlockSpec((tm, D_MODEL), lambda i: (i, 0)),
        out_shape=jax.ShapeDtypeStruct((t, D_MODEL), F32),
        compiler_params=_params(1),
        name="final_norm",
    )(x2, g)


def _router_weights(w):
    wp = jnp.pad(w, ((0, 0), (0, ROUTER_PAD - N_EXPERTS)))
    hi = wp.astype(BF16)
    lo = (wp - hi.astype(F32)).astype(BF16)
    return jnp.concatenate([hi, lo], axis=1)


def kernel(x, g_mix, w_in, conv_w, w_branch_conv, w_branch_attn, w_out, g_ffn, w_ffn_gate, w_ffn_up, w_ffn_down, w_router, w_exp_gate, w_exp_up, w_exp_down, g_final):
    b, s, d = x.shape
    depth = g_mix.shape[0]
    t = b * s
    x2 = x.reshape(t, d)
    gf = g_final.reshape(1, d)
    for i in range(depth):
        last = i == depth - 1
        moe = i % 2 == 1
        j = i // 2
        qz, k, vz, ma, gbs = _mix_in(x2, g_mix[i].reshape(1, d), w_in[i].astype(BF16), conv_w[i],
                                     w_branch_conv[i].astype(BF16), s)
        ub = _attention(qz.reshape(b, s, D_SPLIT), k.reshape(b, s, D_ATTN),
                        vz.reshape(b, s, D_SPLIT)).reshape(t, D_ATTN)
        wb = w_branch_attn[i].astype(BF16)
        wo = w_out[i].astype(BF16)
        gi = g_ffn[i].reshape(1, d)
        if not moe:
            x2 = _mix_out_dense(x2, ma, gbs, ub, wb, wo, gi, w_ffn_gate[j].astype(BF16),
                                w_ffn_up[j].astype(BF16), w_ffn_down[j].astype(BF16))
            if last:
                x2 = _final_norm(x2, gf)
        else:
            x2, h, logits = _mix_out_router(x2, ma, gbs, ub, wb, wo, gi,
                                            _router_weights(w_router[j]))
            tm = min(512, t)
            top_w, row_token, row_of, tile_expert, n_valid, n_tiles = _route(
                logits[:, :N_EXPERTS], tm)
            y = _experts(tile_expert, n_valid, row_token, h,
                         w_exp_gate[j].astype(BF16), w_exp_up[j].astype(BF16),
                         w_exp_down[j].astype(BF16), tm, n_tiles)
            x2 = _combine(row_of, x2, y, top_w, gf, final_norm=last)
    return x2.reshape(b, s, d)
```

```python
import functools
import math

import jax
import jax.numpy as jnp
from jax import lax
from jax.experimental import pallas as pl
from jax.experimental.pallas import tpu as pltpu

F32 = jnp.float32
BF16 = jnp.bfloat16

D_MODEL = 1024
D_CONV = 512
D_ATTN = 512
N_HEADS = 8
HEAD_DIM = 64
N_EXPERTS = 8
TOP_K = 2
EPS = 1e-6
CONV_WIDTH = 3

LANES = 128
VMEM_LIMIT_BYTES = 56 * 1024 * 1024

ATTN_BLOCK = 128
ATTN_GROUP = 4
N_PAIRS = N_HEADS // 2
D_SPLIT = 2 * D_ATTN
LOG2_F32_ZERO = -151.0
ROUTER_PAD = 128
ROW_CHUNKS = D_MODEL // LANES


def _params(n_grid, vmem=VMEM_LIMIT_BYTES):
    return pltpu.CompilerParams(
        dimension_semantics=("arbitrary",) * n_grid, vmem_limit_bytes=vmem)


def _rms(x, g):
    return x * lax.rsqrt(jnp.mean(x * x, axis=-1, keepdims=True) + EPS) * g


def _dot(a, b):
    return jnp.dot(a, b, preferred_element_type=F32)


def _const_spec(shape):
    return pl.BlockSpec(shape, lambda i: (0,) * len(shape), pipeline_mode=pl.Buffered(1))


def _mix_in_body(x_ref, g_ref, win_ref, cw_ref, wa_ref,
                 qz_ref, k_ref, vz_ref, ma_ref, gbs_ref, ubuf, *, tm, tiles_per_seq):
    i = pl.program_id(0)
    h = _rms(x_ref[...], g_ref[...]).astype(BF16)

    p = _dot(h, win_ref[:, 0:3 * D_CONV])
    u = p[:, 2 * D_CONV:3 * D_CONV] * p[:, 0:D_CONV]

    @pl.when(i % tiles_per_seq == 0)
    def _():
        ubuf[0:8, :] = jnp.zeros((8, D_CONV), F32)

    ubuf[8:8 + tm, :] = u
    cw = cw_ref[...]
    conv = (cw[0:1, :] * ubuf[6:6 + tm, :] + cw[1:2, :] * ubuf[7:7 + tm, :]) + cw[2:3, :] * u
    ubuf[0:8, :] = ubuf[tm:tm + 8, :]
    ua = (p[:, D_CONV:2 * D_CONV] * conv).astype(BF16)

    o = 3 * D_CONV
    qkv = _dot(h, win_ref[:, o:o + 3 * D_ATTN])
    o += 3 * D_ATTN
    ga = _dot(h, win_ref[:, o:o + D_MODEL])
    gb = _dot(h, win_ref[:, o + D_MODEL:o + 2 * D_MODEL])

    first = lax.broadcasted_iota(jnp.int32, (tm, LANES), 1) < HEAD_DIM
    q_scale = (HEAD_DIM ** -0.5) * math.log2(math.e)
    for pr in range(N_PAIRS):
        qs = qkv[:, pr * LANES:(pr + 1) * LANES] * q_scale
        vs = qkv[:, 2 * D_ATTN + pr * LANES:2 * D_ATTN + (pr + 1) * LANES]
        lo, mid, hi = 2 * pr * LANES, (2 * pr + 1) * LANES, (2 * pr + 2) * LANES
        qz_ref[:, lo:mid] = jnp.where(first, qs, 0.0).astype(BF16)
        qz_ref[:, mid:hi] = jnp.where(first, 0.0, qs).astype(BF16)
        vz_ref[:, lo:mid] = jnp.where(first, vs, 0.0).astype(BF16)
        vz_ref[:, mid:hi] = jnp.where(first, 0.0, vs).astype(BF16)
    k_ref[...] = qkv[:, D_ATTN:2 * D_ATTN].astype(BF16)
    gbs_ref[...] = jax.nn.sigmoid(gb).astype(BF16)
    ma_ref[...] = (jax.nn.sigmoid(ga) * _dot(ua, wa_ref[...])).astype(BF16)


def _mix_in(x2, g, win, cw, wa, seq):
    t = x2.shape[0]
    tm = min(512, seq)
    n_in = win.shape[1]
    body = functools.partial(_mix_in_body, tm=tm, tiles_per_seq=seq // tm)
    row = lambda w: pl.BlockSpec((tm, w), lambda i: (i, 0))
    return pl.pallas_call(
        body,
        grid=(t // tm,),
        in_specs=[row(D_MODEL), _const_spec((1, D_MODEL)), _const_spec((D_MODEL, n_in)),
                  _const_spec((CONV_WIDTH, D_CONV)), _const_spec((D_CONV, D_MODEL))],
        out_specs=[row(D_SPLIT), row(D_ATTN), row(D_SPLIT), row(D_MODEL), row(D_MODEL)],
        out_shape=[jax.ShapeDtypeStruct((t, D_SPLIT), BF16),
                   jax.ShapeDtypeStruct((t, D_ATTN), BF16),
                   jax.ShapeDtypeStruct((t, D_SPLIT), BF16),
                   jax.ShapeDtypeStruct((t, D_MODEL), BF16),
                   jax.ShapeDtypeStruct((t, D_MODEL), BF16)],
        scratch_shapes=[pltpu.VMEM((tm + 8, D_CONV), F32)],
        compiler_params=_params(1),
        name="mix_in",
    )(x2, g, win, cw, wa)


def _attn_body(qz_ref, k_ref, vz_ref, o_ref, r_scr, o_scr):
    i = pl.program_id(1)
    tb = ATTN_BLOCK
    row2 = lax.broadcasted_iota(jnp.int32, (2 * tb, tb), 0)
    col2 = lax.broadcasted_iota(jnp.int32, (2 * tb, tb), 1)
    causal2 = col2 < jnp.where(row2 >= tb, row2 - tb, row2)
    rj = lax.broadcasted_iota(jnp.int32, (tb, 2 * tb), 0)
    cs_ = lax.broadcasted_iota(jnp.int32, (tb, 2 * tb), 1)
    suffix = jnp.where(jnp.logical_or(rj > cs_, cs_ >= tb), 1.0, 0.0).astype(BF16)

    r_scr[...] = jnp.zeros_like(r_scr)
    o_scr[...] = jnp.zeros_like(o_scr)

    def visit(n, diag, groups):
        chains = [(g, p) for g in groups for p in range(N_PAIRS)]
        start = {g: pl.multiple_of((ATTN_GROUP * i + g - n) * tb, tb) for g in groups}
        zs = []
        for g, p in chains:
            rows = slice(g * tb, (g + 1) * tb)
            q2 = jnp.concatenate([qz_ref[0, rows, 2 * p * LANES:(2 * p + 1) * LANES],
                                  qz_ref[0, rows, (2 * p + 1) * LANES:(2 * p + 2) * LANES]], axis=0)
            kp = k_ref[0, pl.ds(start[g], tb), p * LANES:(p + 1) * LANES]
            zs.append(lax.dot_general(q2, kp, (((1,), (1,)), ((), ())),
                                      preferred_element_type=F32))
        log_betas, log_1ms = [], []
        for z in zs:
            sp = jnp.log2(1.0 + jnp.exp2(-jnp.abs(z)))
            log_beta = jnp.minimum(z, 0.0) - sp
            log_1m = log_beta - z
            if diag:
                log_1m = jnp.where(causal2, log_1m, 0.0)
            log_betas.append(log_beta)
            log_1ms.append(log_1m.astype(BF16))
        c_all = _dot(jnp.concatenate(log_1ms, axis=0), suffix)
        rmax = None
        for ci, (g, p) in enumerate(chains):
            idx = g * N_PAIRS + p
            c = c_all[ci * 2 * tb:(ci + 1) * 2 * tb]
            r_old = r_scr[idx]
            a = jnp.exp2(log_betas[ci] + c[:, 0:tb] + r_old)
            if diag:
                a = jnp.where(causal2, a, 0.0)
            r_new = r_old + c[:, tb:2 * tb]
            r_scr[idx] = r_new
            rmax = r_new if rmax is None else jnp.maximum(rmax, r_new)
            a2 = jnp.concatenate([a[0:tb], a[tb:2 * tb]], axis=1).astype(BF16)
            v2 = jnp.concatenate(
                [vz_ref[0, pl.ds(start[g], tb), 2 * p * LANES:(2 * p + 1) * LANES],
                 vz_ref[0, pl.ds(start[g], tb), (2 * p + 1) * LANES:(2 * p + 2) * LANES]], axis=0)
            o_scr[idx] += _dot(a2, v2)
        return jnp.max(rmax, axis=0, keepdims=True)[0, 0]

    all_groups = list(range(ATTN_GROUP))
    m0 = visit(0, True, all_groups)

    def cond(c):
        n, m = c
        return jnp.logical_and(n <= ATTN_GROUP * i, m > LOG2_F32_ZERO)

    def body(c):
        n, _ = c
        return n + 1, visit(n, False, all_groups)

    n_end, m_end = lax.while_loop(cond, body, (jnp.int32(1), m0))

    for extra in range(1, ATTN_GROUP):
        @pl.when(jnp.logical_and(n_end == ATTN_GROUP * i + 1, m_end > LOG2_F32_ZERO))
        def _():
            visit(ATTN_GROUP * i + extra, False, all_groups[extra:])

    for g in range(ATTN_GROUP):
        for p in range(N_PAIRS):
            o_ref[0, g * tb:(g + 1) * tb, p * LANES:(p + 1) * LANES] = (
                o_scr[g * N_PAIRS + p].astype(o_ref.dtype))


def _attention(qz, k, vz):
    b, s, _ = k.shape
    tb = ATTN_BLOCK
    tq = ATTN_GROUP * tb
    n_chains = ATTN_GROUP * N_PAIRS
    return pl.pallas_call(
        _attn_body,
        grid=(b, s // tq),
        in_specs=[pl.BlockSpec((1, tq, D_SPLIT), lambda bi, i: (bi, i, 0)),
                  pl.BlockSpec((1, s, D_ATTN), lambda bi, i: (bi, 0, 0)),
                  pl.BlockSpec((1, s, D_SPLIT), lambda bi, i: (bi, 0, 0))],
        out_specs=pl.BlockSpec((1, tq, D_ATTN), lambda bi, i: (bi, i, 0)),
        out_shape=jax.ShapeDtypeStruct((b, s, D_ATTN), BF16),
        scratch_shapes=[pltpu.VMEM((n_chains, 2 * tb, tb), F32),
                        pltpu.VMEM((n_chains, tb, LANES), F32)],
        compiler_params=_params(2),
        name="stickbreak_attn",
    )(qz, k, vz)


def _merge(x_ref, ma_ref, gbs_ref, ub_ref, wb_ref, wo_ref, g_ref):
    mix = ma_ref[...].astype(F32) + gbs_ref[...].astype(F32) * _dot(ub_ref[...], wb_ref[...])
    xn = x_ref[...] + _dot(mix.astype(BF16), wo_ref[...])
    return xn, _rms(xn, g_ref[...])


def _mix_out_dense_body(x_ref, ma_ref, gbs_ref, ub_ref, wb_ref, wo_ref, g_ref,
                        wg_ref, wu_ref, wd_ref, o_ref):
    xn, h = _merge(x_ref, ma_ref, gbs_ref, ub_ref, wb_ref, wo_ref, g_ref)
    h = h.astype(BF16)
    act = (jax.nn.silu(_dot(h, wg_ref[...])) * _dot(h, wu_ref[...])).astype(BF16)
    o_ref[...] = xn + _dot(act, wd_ref[...])


def _mix_out_router_body(x_ref, ma_ref, gbs_ref, ub_ref, wb_ref, wo_ref, g_ref, wr_ref,
                         xo_ref, h_ref, lg_ref, *, tm):
    xn, h = _merge(x_ref, ma_ref, gbs_ref, ub_ref, wb_ref, wo_ref, g_ref)
    xo_ref[...] = xn
    for c in range(ROW_CHUNKS):
        h_ref[pl.ds(c, tm, stride=ROW_CHUNKS), :] = h[:, c * LANES:(c + 1) * LANES]
    h_hi = h.astype(BF16)
    h_lo = (h - h_hi.astype(F32)).astype(BF16)
    c2 = _dot(h_hi, wr_ref[...])
    lg_ref[...] = (c2[:, 0:ROUTER_PAD] + c2[:, ROUTER_PAD:2 * ROUTER_PAD]
                   + _dot(h_lo, wr_ref[:, 0:ROUTER_PAD]))


def _mix_out_specs(tm):
    row = lambda w: pl.BlockSpec((tm, w), lambda i: (i, 0))
    return [row(D_MODEL), row(D_MODEL), row(D_MODEL), row(D_ATTN),
            _const_spec((D_ATTN, D_MODEL)), _const_spec((D_MODEL, D_MODEL)),
            _const_spec((1, D_MODEL))]


def _mix_out_dense(x2, ma, gbs, ub, wb, wo, g, wg, wu, wd):
    t = x2.shape[0]
    tm = min(512, t)
    dff = wg.shape[1]
    return pl.pallas_call(
        _mix_out_dense_body,
        grid=(t // tm,),
        in_specs=_mix_out_specs(tm) + [_const_spec((D_MODEL, dff)), _const_spec((D_MODEL, dff)),
                                       _const_spec((dff, D_MODEL))],
        out_specs=pl.BlockSpec((tm, D_MODEL), lambda i: (i, 0)),
        out_shape=jax.ShapeDtypeStruct((t, D_MODEL), F32),
        compiler_params=_params(1),
        name="mix_out_dense_ffn",
    )(x2, ma, gbs, ub, wb, wo, g, wg, wu, wd)


def _mix_out_router(x2, ma, gbs, ub, wb, wo, g, wr):
    t = x2.shape[0]
    tm = min(512, t)
    row = lambda w: pl.BlockSpec((tm, w), lambda i: (i, 0))
    return pl.pallas_call(
        functools.partial(_mix_out_router_body, tm=tm),
        grid=(t // tm,),
        in_specs=_mix_out_specs(tm) + [_const_spec((D_MODEL, 2 * ROUTER_PAD))],
        out_specs=[row(D_MODEL), pl.BlockSpec((tm * ROW_CHUNKS, LANES), lambda i: (i, 0)),
                   row(ROUTER_PAD)],
        out_shape=[jax.ShapeDtypeStruct((t, D_MODEL), F32),
                   jax.ShapeDtypeStruct((t * ROW_CHUNKS, LANES), F32),
                   jax.ShapeDtypeStruct((t, ROUTER_PAD), F32)],
        compiler_params=_params(1),
        name="mix_out_router",
    )(x2, ma, gbs, ub, wb, wo, g, wr)


def _route(logits, tm):
    t = logits.shape[0]
    top_vals, top_idx = lax.top_k(logits, TOP_K)
    top_w = jax.nn.softmax(top_vals, axis=-1)
    e_flat = top_idx.T.reshape(-1).astype(jnp.int32)
    experts = jnp.arange(N_EXPERTS, dtype=jnp.int32)
    onehot = (e_flat[:, None] == experts[None, :]).astype(jnp.int32)
    csum = jnp.cumsum(onehot, axis=0)
    counts = csum[-1]
    starts = jnp.cumsum(counts) - counts
    pos = jnp.sum((csum - 1 + starts[None, :]) * onehot, axis=1).astype(jnp.int32)
    order = _invert_permutation(pos)
    tiles_e = (counts + tm - 1) // tm
    tile_end = jnp.cumsum(tiles_e)
    tile_begin = tile_end - tiles_e
    n_tiles = (TOP_K * t) // tm + N_EXPERTS
    ti = jnp.arange(n_tiles, dtype=jnp.int32)
    tile_expert = jnp.minimum(
        jnp.sum((ti[:, None] >= tile_end[None, :]).astype(jnp.int32), axis=1), N_EXPERTS - 1)
    local = ti - tile_begin[tile_expert]
    row0 = starts[tile_expert] + local * tm
    n_rows = jnp.clip(counts[tile_expert] - local * tm, 0, tm)
    r = jnp.arange(tm, dtype=jnp.int32)[None, :]
    a = order[jnp.minimum(row0[:, None] + r, TOP_K * t - 1)]
    row_token = jnp.where(r < n_rows[:, None], jnp.where(a >= t, a - t, a), 0).reshape(-1)
    row_of = jnp.sum((csum - 1 + (tile_begin * tm)[None, :]) * onehot, axis=1).astype(jnp.int32)
    n_valid = tile_end[-1].astype(jnp.int32).reshape(1)
    return top_w, row_token.astype(jnp.int32), row_of, tile_expert.astype(jnp.int32), n_valid, n_tiles


def _invert_body(pos_ref, order_ref, *, n):
    def body(a, c):
        order_ref[pos_ref[a]] = a
        return c
    lax.fori_loop(0, n, body, 0, unroll=16)


def _invert_permutation(pos):
    n = pos.shape[0]
    return pl.pallas_call(
        functools.partial(_invert_body, n=n),
        in_specs=[pl.BlockSpec(memory_space=pltpu.SMEM)],
        out_specs=pl.BlockSpec(memory_space=pltpu.SMEM),
        out_shape=jax.ShapeDtypeStruct((n,), jnp.int32),
        name="moe_invert",
    )(pos)


def _row_tile(r):
    return pl.ds(pl.multiple_of(r * ROW_CHUNKS, ROW_CHUNKS), ROW_CHUNKS)


def _row_dma_loop(n_rows, issue):
    def body(g, c):
        for j in range(8):
            issue(g * 8 + j, j % 2)
        return c
    lax.fori_loop(0, n_rows // 8, body, 0)


def _expert_body(te_ref, nv_ref, tok_ref, h_ref, wg_ref, wu_ref, wd_ref, y_ref, xbuf, gsem,
                 *, tm):
    del te_ref
    i = pl.program_id(0)
    n_valid = nv_ref[0]
    slot = i % 2
    rc = ROW_CHUNKS

    def gather_start(tile, sl):
        def issue(r, queue):
            pltpu.make_async_copy(h_ref.at[_row_tile(tok_ref[tile * tm + r])],
                                  xbuf.at[sl, _row_tile(r)], gsem.at[sl]).start(priority=queue)
        _row_dma_loop(tm, issue)

    @pl.when(jnp.logical_and(i == 0, n_valid > 0))
    def _():
        gather_start(0, 0)

    @pl.when(i + 1 < n_valid)
    def _():
        gather_start(i + 1, 1 - slot)

    @pl.when(i < n_valid)
    def _():
        pltpu.make_async_copy(h_ref.at[pl.ds(0, tm * rc)], xbuf.at[slot], gsem.at[slot]).wait()
        x = jnp.concatenate(
            [xbuf[slot, pl.ds(c, tm, stride=rc), :].astype(BF16) for c in range(rc)], axis=1)
        act = (jax.nn.silu(_dot(x, wg_ref[0])) * _dot(x, wu_ref[0])).astype(BF16)
        y = _dot(act, wd_ref[0])
        for c in range(rc):
            y_ref[pl.ds(c, tm, stride=rc), :] = y[:, c * LANES:(c + 1) * LANES]

    @pl.when(i >= n_valid)
    def _():
        y_ref[...] = jnp.zeros_like(y_ref)


def _experts(tile_expert, n_valid, row_token, h, wg, wu, wd, tm, n_tiles):
    dfe = wg.shape[2]
    wmap = lambda i, te, nv, tk: (te[i], 0, 0)
    return pl.pallas_call(
        functools.partial(_expert_body, tm=tm),
        grid_spec=pltpu.PrefetchScalarGridSpec(
            num_scalar_prefetch=3,
            grid=(n_tiles,),
            in_specs=[pl.BlockSpec(memory_space=pl.ANY),
                      pl.BlockSpec((1, D_MODEL, dfe), wmap),
                      pl.BlockSpec((1, D_MODEL, dfe), wmap),
                      pl.BlockSpec((1, dfe, D_MODEL), wmap)],
            out_specs=pl.BlockSpec((tm * ROW_CHUNKS, LANES), lambda i, te, nv, tk: (i, 0)),
            scratch_shapes=[pltpu.VMEM((2, tm * ROW_CHUNKS, LANES), F32),
                            pltpu.SemaphoreType.DMA((2,))],
        ),
        out_shape=jax.ShapeDtypeStruct((n_tiles * tm * ROW_CHUNKS, LANES), F32),
        compiler_params=_params(1),
        name="moe_experts",
    )(tile_expert, n_valid, row_token, h, wg, wu, wd)


def _combine_body(row_ref, x_ref, w_ref, g_ref, y_ref, o_ref, ybuf, sem,
                  *, tm, t, n_steps, final_norm):
    i = pl.program_id(0)
    slot = i % 2
    rc = ROW_CHUNKS

    def gather_start(step, sl):
        def issue(r, queue):
            for k in range(TOP_K):
                pltpu.make_async_copy(y_ref.at[_row_tile(row_ref[k * t + step * tm + r])],
                                      ybuf.at[sl, _row_tile(k * tm + r)],
                                      sem.at[sl]).start(priority=(queue + k) % 2)
        _row_dma_loop(tm, issue)

    @pl.when(i == 0)
    def _():
        gather_start(0, 0)

    @pl.when(i + 1 < n_steps)
    def _():
        gather_start(i + 1, 1 - slot)

    pltpu.make_async_copy(y_ref.at[pl.ds(0, TOP_K * tm * rc)], ybuf.at[slot], sem.at[slot]).wait()
    rows = lambda k: jnp.concatenate(
        [ybuf[slot, pl.ds(k * tm * rc + c, tm, stride=rc), :] for c in range(rc)], axis=1)
    w = w_ref[...]
    xn = x_ref[...] + (w[:, 0:1] * rows(0) + w[:, 1:2] * rows(1))
    o_ref[...] = _rms(xn, g_ref[...]) if final_norm else xn


def _combine(row_of, x2, y, top_w, g, final_norm):
    t = x2.shape[0]
    tm = min(512, t)
    n_steps = t // tm
    return pl.pallas_call(
        functools.partial(_combine_body, tm=tm, t=t, n_steps=n_steps, final_norm=final_norm),
        grid_spec=pltpu.PrefetchScalarGridSpec(
            num_scalar_prefetch=1,
            grid=(n_steps,),
            in_specs=[pl.BlockSpec((tm, D_MODEL), lambda i, ro: (i, 0)),
                      pl.BlockSpec((tm, TOP_K), lambda i, ro: (i, 0)),
                      pl.BlockSpec((1, D_MODEL), lambda i, ro: (0, 0)),
                      pl.BlockSpec(memory_space=pl.ANY)],
            out_specs=pl.BlockSpec((tm, D_MODEL), lambda i, ro: (i, 0)),
            scratch_shapes=[pltpu.VMEM((2, TOP_K * tm * ROW_CHUNKS, LANES), F32),
                            pltpu.SemaphoreType.DMA((2,))],
        ),
        out_shape=jax.ShapeDtypeStruct((t, D_MODEL), F32),
        compiler_params=_params(1),
        name="moe_combine",
    )(row_of, x2, top_w, g, y)


def _final_norm_body(x_ref, g_ref, o_ref):
    o_ref[...] = _rms(x_ref[...], g_ref[...])


def _final_norm(x2, g):
    t = x2.shape[0]
    tm = min(512, t)
    return pl.pallas_call(
        _final_norm_body,
        grid=(t // tm,),
        in_specs=[pl.BlockSpec((tm, D_MODEL), lambda i: (i, 0)),
                  pl.BlockSpec((1, D_MODEL), lambda i: (0, 0))],
        out_specs=pl.BlockSpec((tm, D_MODEL), lambda i: (i, 0)),
        out_shape=jax.ShapeDtypeStruct((t, D_MODEL), F32),
        compiler_params=_params(1),
        name="final_norm",
    )(x2, g)


def _router_weights(w):
    wp = jnp.pad(w, ((0, 0), (0, ROUTER_PAD - N_EXPERTS)))
    hi = wp.astype(BF16)
    lo = (wp - hi.astype(F32)).astype(BF16)
    return jnp.concatenate([hi, lo], axis=1)


def kernel(x, g_mix, w_in, conv_w, w_branch_conv, w_branch_attn, w_out, g_ffn, w_ffn_gate, w_ffn_up, w_ffn_down, w_router, w_exp_gate, w_exp_up, w_exp_down, g_final):
    b, s, d = x.shape
    depth = g_mix.shape[0]
    t = b * s
    x2 = x.reshape(t, d)
    gf = g_final.reshape(1, d)
    for i in range(depth):
        last = i == depth - 1
        moe = i % 2 == 1
        j = i // 2
        qz, k, vz, ma, gbs = _mix_in(x2, g_mix[i].reshape(1, d), w_in[i].astype(BF16), conv_w[i],
                                     w_branch_conv[i].astype(BF16), s)
        ub = _attention(qz.reshape(b, s, D_SPLIT), k.reshape(b, s, D_ATTN),
                        vz.reshape(b, s, D_SPLIT)).reshape(t, D_ATTN)
        wb = w_branch_attn[i].astype(BF16)
        wo = w_out[i].astype(BF16)
        gi = g_ffn[i].reshape(1, d)
        if not moe:
            x2 = _mix_out_dense(x2, ma, gbs, ub, wb, wo, gi, w_ffn_gate[j].astype(BF16),
                                w_ffn_up[j].astype(BF16), w_ffn_down[j].astype(BF16))
            if last:
                x2 = _final_norm(x2, gf)
        else:
            x2, h, logits = _mix_out_router(x2, ma, gbs, ub, wb, wo, gi,
                                            _router_weights(w_router[j]))
            tm = min(512, t)
            top_w, row_token, row_of, tile_expert, n_valid, n_tiles = _route(
                logits[:, :N_EXPERTS], tm)
            y = _experts(tile_expert, n_valid, row_token, h,
                         w_exp_gate[j].astype(BF16), w_exp_up[j].astype(BF16),
                         w_exp_down[j].astype(BF16), tm, n_tiles)
            x2 = _combine(row_of, x2, y, top_w, gf, final_norm=last)
    return x2.reshape(b, s, d)
```

```python
import functools
import math

import jax
import jax.numpy as jnp
from jax import lax
from jax.experimental import pallas as pl
from jax.experimental.pallas import tpu as pltpu

F32 = jnp.float32
BF16 = jnp.bfloat16

D_MODEL = 1024
D_CONV = 512
D_ATTN = 512
N_HEADS = 8
HEAD_DIM = 64
N_EXPERTS = 8
TOP_K = 2
EPS = 1e-6
CONV_WIDTH = 3

LANES = 128
VMEM_LIMIT_BYTES = 56 * 1024 * 1024

ATTN_BLOCK = 128
ATTN_GROUP = 4
ATTN_FUSED_STEPS = 3
N_PAIRS = N_HEADS // 2
D_SPLIT = 2 * D_ATTN
LOG2_F32_ZERO = -151.0
ROUTER_PAD = 128
ROW_CHUNKS = D_MODEL // LANES


def _params(n_grid, vmem=VMEM_LIMIT_BYTES):
    return pltpu.CompilerParams(
        dimension_semantics=("arbitrary",) * n_grid, vmem_limit_bytes=vmem)


def _rms(x, g):
    return x * lax.rsqrt(jnp.mean(x * x, axis=-1, keepdims=True) + EPS) * g


def _dot(a, b):
    return jnp.dot(a, b, preferred_element_type=F32)


def _const_spec(shape):
    return pl.BlockSpec(shape, lambda i: (0,) * len(shape), pipeline_mode=pl.Buffered(1))


def _mix_in_body(x_ref, g_ref, win_ref, cw_ref, wa_ref,
                 qz_ref, k_ref, vz_ref, ma_ref, gbs_ref, ubuf, *, tm, tiles_per_seq):
    i = pl.program_id(0)
    h = _rms(x_ref[...], g_ref[...]).astype(BF16)

    p = _dot(h, win_ref[:, 0:3 * D_CONV])
    u = p[:, 2 * D_CONV:3 * D_CONV] * p[:, 0:D_CONV]

    @pl.when(i % tiles_per_seq == 0)
    def _():
        ubuf[0:8, :] = jnp.zeros((8, D_CONV), F32)

    ubuf[8:8 + tm, :] = u
    cw = cw_ref[...]
    conv = (cw[0:1, :] * ubuf[6:6 + tm, :] + cw[1:2, :] * ubuf[7:7 + tm, :]) + cw[2:3, :] * u
    ubuf[0:8, :] = ubuf[tm:tm + 8, :]
    ua = (p[:, D_CONV:2 * D_CONV] * conv).astype(BF16)

    o = 3 * D_CONV
    qkv = _dot(h, win_ref[:, o:o + 3 * D_ATTN])
    o += 3 * D_ATTN
    ga = _dot(h, win_ref[:, o:o + D_MODEL])
    gb = _dot(h, win_ref[:, o + D_MODEL:o + 2 * D_MODEL])

    first = lax.broadcasted_iota(jnp.int32, (tm, LANES), 1) < HEAD_DIM
    q_scale = (HEAD_DIM ** -0.5) * math.log2(math.e)
    for pr in range(N_PAIRS):
        qs = qkv[:, pr * LANES:(pr + 1) * LANES] * q_scale
        vs = qkv[:, 2 * D_ATTN + pr * LANES:2 * D_ATTN + (pr + 1) * LANES]
        lo, mid, hi = 2 * pr * LANES, (2 * pr + 1) * LANES, (2 * pr + 2) * LANES
        qz_ref[:, lo:mid] = jnp.where(first, qs, 0.0).astype(BF16)
        qz_ref[:, mid:hi] = jnp.where(first, 0.0, qs).astype(BF16)
        vz_ref[:, lo:mid] = jnp.where(first, vs, 0.0).astype(BF16)
        vz_ref[:, mid:hi] = jnp.where(first, 0.0, vs).astype(BF16)
    k_ref[...] = qkv[:, D_ATTN:2 * D_ATTN].astype(BF16)
    gbs_ref[...] = jax.nn.sigmoid(gb).astype(BF16)
    ma_ref[...] = (jax.nn.sigmoid(ga) * _dot(ua, wa_ref[...])).astype(BF16)


def _mix_in(x2, g, win, cw, wa, seq):
    t = x2.shape[0]
    tm = min(512, seq)
    n_in = win.shape[1]
    body = functools.partial(_mix_in_body, tm=tm, tiles_per_seq=seq // tm)
    row = lambda w: pl.BlockSpec((tm, w), lambda i: (i, 0))
    return pl.pallas_call(
        body,
        grid=(t // tm,),
        in_specs=[row(D_MODEL), _const_spec((1, D_MODEL)), _const_spec((D_MODEL, n_in)),
                  _const_spec((CONV_WIDTH, D_CONV)), _const_spec((D_CONV, D_MODEL))],
        out_specs=[row(D_SPLIT), row(D_ATTN), row(D_SPLIT), row(D_MODEL), row(D_MODEL)],
        out_shape=[jax.ShapeDtypeStruct((t, D_SPLIT), BF16),
                   jax.ShapeDtypeStruct((t, D_ATTN), BF16),
                   jax.ShapeDtypeStruct((t, D_SPLIT), BF16),
                   jax.ShapeDtypeStruct((t, D_MODEL), BF16),
                   jax.ShapeDtypeStruct((t, D_MODEL), BF16)],
        scratch_shapes=[pltpu.VMEM((tm + 8, D_CONV), F32)],
        compiler_params=_params(1),
        name="mix_in",
    )(x2, g, win, cw, wa)


def _attn_body(qz_ref, k_ref, vz_ref, o_ref, r_scr, o_scr, m_scr):
    i = pl.program_id(1)
    tb = ATTN_BLOCK
    row2 = lax.broadcasted_iota(jnp.int32, (2 * tb, tb), 0)
    col2 = lax.broadcasted_iota(jnp.int32, (2 * tb, tb), 1)
    causal2 = col2 < jnp.where(row2 >= tb, row2 - tb, row2)
    rj = lax.broadcasted_iota(jnp.int32, (tb, 2 * tb), 0)
    cs_ = lax.broadcasted_iota(jnp.int32, (tb, 2 * tb), 1)
    suffix = jnp.where(jnp.logical_or(rj > cs_, cs_ >= tb), 1.0, 0.0).astype(BF16)

    def sweep(steps, groups, fresh):
        chains = [(g, p) for g in groups for p in range(N_PAIRS)]
        start = {(g, s): pl.multiple_of((ATTN_GROUP * i + g - n) * tb, tb)
                 for g in groups for s, n in enumerate(steps)}
        diag = [isinstance(n, int) and n == 0 for n in steps]
        zs, log_betas, log_1ms, cs = {}, {}, {}, {}
        rs = {ci: (None if fresh else r_scr[g * N_PAIRS + p]) for ci, (g, p) in enumerate(chains)}
        os_ = {ci: (None if fresh else o_scr[g * N_PAIRS + p]) for ci, (g, p) in enumerate(chains)}

        def scores(s):
            for ci, (g, p) in enumerate(chains):
                rows = slice(g * tb, (g + 1) * tb)
                q2 = jnp.concatenate(
                    [qz_ref[0, rows, 2 * p * LANES:(2 * p + 1) * LANES],
                     qz_ref[0, rows, (2 * p + 1) * LANES:(2 * p + 2) * LANES]], axis=0)
                kp = k_ref[0, pl.ds(start[g, s], tb), p * LANES:(p + 1) * LANES]
                zs[s, ci] = lax.dot_general(q2, kp, (((1,), (1,)), ((), ())),
                                            preferred_element_type=F32)

        def logs_and_suffix_sums(s):
            parts = []
            for ci in range(len(chains)):
                z = zs[s, ci]
                sp = jnp.log2(1.0 + jnp.exp2(-jnp.abs(z)))
                log_beta = jnp.minimum(z, 0.0) - sp
                log_1m = log_beta - z
                if diag[s]:
                    log_1m = jnp.where(causal2, log_1m, 0.0)
                log_betas[s, ci] = log_beta
                parts.append(log_1m.astype(BF16))
            cs[s] = _dot(jnp.concatenate(parts, axis=0), suffix)

        def weights_and_values(s):
            for ci, (g, p) in enumerate(chains):
                c = cs[s][ci * 2 * tb:(ci + 1) * 2 * tb]
                r = rs[ci]
                t = log_betas[s, ci] + c[:, 0:tb]
                a = jnp.exp2(t if r is None else t + r)
                if diag[s]:
                    a = jnp.where(causal2, a, 0.0)
                rs[ci] = c[:, tb:2 * tb] if r is None else r + c[:, tb:2 * tb]
                a2 = jnp.concatenate([a[0:tb], a[tb:2 * tb]], axis=1).astype(BF16)
                v2 = jnp.concatenate(
                    [vz_ref[0, pl.ds(start[g, s], tb), 2 * p * LANES:(2 * p + 1) * LANES],
                     vz_ref[0, pl.ds(start[g, s], tb), (2 * p + 1) * LANES:(2 * p + 2) * LANES]],
                    axis=0)
                pv = _dot(a2, v2)
                os_[ci] = pv if os_[ci] is None else os_[ci] + pv

        scores(0)
        for s in range(len(steps)):
            if s + 1 < len(steps):
                scores(s + 1)
            logs_and_suffix_sums(s)
            if s >= 1:
                weights_and_values(s - 1)
        weights_and_values(len(steps) - 1)

        rmax = None
        for ci, (g, p) in enumerate(chains):
            r_scr[g * N_PAIRS + p] = rs[ci]
            o_scr[g * N_PAIRS + p] = os_[ci]
            rmax = rs[ci] if rmax is None else jnp.maximum(rmax, rs[ci])
        return jnp.max(rmax, axis=0, keepdims=True)[0, 0]

    all_groups = list(range(ATTN_GROUP))
    fused = list(range(min(ATTN_FUSED_STEPS, ATTN_GROUP + 1)))

    def tails(first_extra):
        for extra in range(first_extra, ATTN_GROUP):
            sweep([ATTN_GROUP * i + extra], all_groups[extra:], False)

    @pl.when(i == 0)
    def _():
        m_scr[0] = sweep([0], all_groups, True)

        @pl.when(m_scr[0] > LOG2_F32_ZERO)
        def _():
            tails(1)

    @pl.when(i > 0)
    def _():
        m_scr[0] = sweep(fused, all_groups, True)

    def cond(c):
        n, m = c
        return jnp.logical_and(n <= ATTN_GROUP * i, m > LOG2_F32_ZERO)

    def body(c):
        n, _ = c
        return n + 1, sweep([n], all_groups, False)

    n_end, m_end = lax.while_loop(cond, body, (jnp.int32(len(fused)), m_scr[0]))

    @pl.when(jnp.logical_and(i > 0, jnp.logical_and(n_end == ATTN_GROUP * i + 1,
                                                     m_end > LOG2_F32_ZERO)))
    def _():
        tails(1)

    for g in range(ATTN_GROUP):
        for p in range(N_PAIRS):
            o_ref[0, g * tb:(g + 1) * tb, p * LANES:(p + 1) * LANES] = (
                o_scr[g * N_PAIRS + p].astype(o_ref.dtype))


def _attention(qz, k, vz):
    b, s, _ = k.shape
    tb = ATTN_BLOCK
    tq = ATTN_GROUP * tb
    n_chains = ATTN_GROUP * N_PAIRS
    return pl.pallas_call(
        _attn_body,
        grid=(b, s // tq),
        in_specs=[pl.BlockSpec((1, tq, D_SPLIT), lambda bi, i: (bi, i, 0)),
                  pl.BlockSpec((1, s, D_ATTN), lambda bi, i: (bi, 0, 0)),
                  pl.BlockSpec((1, s, D_SPLIT), lambda bi, i: (bi, 0, 0))],
        out_specs=pl.BlockSpec((1, tq, D_ATTN), lambda bi, i: (bi, i, 0)),
        out_shape=jax.ShapeDtypeStruct((b, s, D_ATTN), BF16),
        scratch_shapes=[pltpu.VMEM((n_chains, 2 * tb, tb), F32),
                        pltpu.VMEM((n_chains, tb, LANES), F32),
                        pltpu.SMEM((1,), F32)],
        compiler_params=_params(2),
        name="stickbreak_attn",
    )(qz, k, vz)


def _merge(x_ref, ma_ref, gbs_ref, ub_ref, wb_ref, wo_ref, g_ref):
    mix = ma_ref[...].astype(F32) + gbs_ref[...].astype(F32) * _dot(ub_ref[...], wb_ref[...])
    xn = x_ref[...] + _dot(mix.astype(BF16), wo_ref[...])
    return xn, _rms(xn, g_ref[...])


def _mix_out_dense_body(x_ref, ma_ref, gbs_ref, ub_ref, wb_ref, wo_ref, g_ref,
                        wg_ref, wu_ref, wd_ref, o_ref):
    xn, h = _merge(x_ref, ma_ref, gbs_ref, ub_ref, wb_ref, wo_ref, g_ref)
    h = h.astype(BF16)
    act = (jax.nn.silu(_dot(h, wg_ref[...])) * _dot(h, wu_ref[...])).astype(BF16)
    o_ref[...] = xn + _dot(act, wd_ref[...])


def _mix_out_router_body(x_ref, ma_ref, gbs_ref, ub_ref, wb_ref, wo_ref, g_ref, wr_ref,
                         xo_ref, h_ref, lg_ref, *, tm):
    xn, h = _merge(x_ref, ma_ref, gbs_ref, ub_ref, wb_ref, wo_ref, g_ref)
    xo_ref[...] = xn
    for c in range(ROW_CHUNKS):
        h_ref[pl.ds(c, tm, stride=ROW_CHUNKS), :] = h[:, c * LANES:(c + 1) * LANES]
    h_hi = h.astype(BF16)
    h_lo = (h - h_hi.astype(F32)).astype(BF16)
    c2 = _dot(h_hi, wr_ref[...])
    lg_ref[...] = (c2[:, 0:ROUTER_PAD] + c2[:, ROUTER_PAD:2 * ROUTER_PAD]
                   + _dot(h_lo, wr_ref[:, 0:ROUTER_PAD]))


def _mix_out_specs(tm):
    row = lambda w: pl.BlockSpec((tm, w), lambda i: (i, 0))
    return [row(D_MODEL), row(D_MODEL), row(D_MODEL), row(D_ATTN),
            _const_spec((D_ATTN, D_MODEL)), _const_spec((D_MODEL, D_MODEL)),
            _const_spec((1, D_MODEL))]


def _mix_out_dense(x2, ma, gbs, ub, wb, wo, g, wg, wu, wd):
    t = x2.shape[0]
    tm = min(512, t)
    dff = wg.shape[1]
    return pl.pallas_call(
        _mix_out_dense_body,
        grid=(t // tm,),
        in_specs=_mix_out_specs(tm) + [_const_spec((D_MODEL, dff)), _const_spec((D_MODEL, dff)),
                                       _const_spec((dff, D_MODEL))],
        out_specs=pl.BlockSpec((tm, D_MODEL), lambda i: (i, 0)),
        out_shape=jax.ShapeDtypeStruct((t, D_MODEL), F32),
        compiler_params=_params(1),
        name="mix_out_dense_ffn",
    )(x2, ma, gbs, ub, wb, wo, g, wg, wu, wd)


def _mix_out_router(x2, ma, gbs, ub, wb, wo, g, wr):
    t = x2.shape[0]
    tm = min(512, t)
    row = lambda w: pl.BlockSpec((tm, w), lambda i: (i, 0))
    return pl.pallas_call(
        functools.partial(_mix_out_router_body, tm=tm),
        grid=(t // tm,),
        in_specs=_mix_out_specs(tm) + [_const_spec((D_MODEL, 2 * ROUTER_PAD))],
        out_specs=[row(D_MODEL), pl.BlockSpec((tm * ROW_CHUNKS, LANES), lambda i: (i, 0)),
                   row(ROUTER_PAD)],
        out_shape=[jax.ShapeDtypeStruct((t, D_MODEL), F32),
                   jax.ShapeDtypeStruct((t * ROW_CHUNKS, LANES), F32),
                   jax.ShapeDtypeStruct((t, ROUTER_PAD), F32)],
        compiler_params=_params(1),
        name="mix_out_router",
    )(x2, ma, gbs, ub, wb, wo, g, wr)


def _route(logits, tm):
    t = logits.shape[0]
    top_vals, top_idx = lax.top_k(logits, TOP_K)
    top_w = jax.nn.softmax(top_vals, axis=-1)
    e_flat = top_idx.T.reshape(-1).astype(jnp.int32)
    experts = jnp.arange(N_EXPERTS, dtype=jnp.int32)
    onehot = (e_flat[:, None] == experts[None, :]).astype(jnp.int32)
    csum = jnp.cumsum(onehot, axis=0)
    counts = csum[-1]
    starts = jnp.cumsum(counts) - counts
    pos = jnp.sum((csum - 1 + starts[None, :]) * onehot, axis=1).astype(jnp.int32)
    order = _invert_permutation(pos)
    tiles_e = (counts + tm - 1) // tm
    tile_end = jnp.cumsum(tiles_e)
    tile_begin = tile_end - tiles_e
    n_tiles = (TOP_K * t) // tm + N_EXPERTS
    ti = jnp.arange(n_tiles, dtype=jnp.int32)
    tile_expert = jnp.minimum(
        jnp.sum((ti[:, None] >= tile_end[None, :]).astype(jnp.int32), axis=1), N_EXPERTS - 1)
    local = ti - tile_begin[tile_expert]
    row0 = starts[tile_expert] + local * tm
    n_rows = jnp.clip(counts[tile_expert] - local * tm, 0, tm)
    r = jnp.arange(tm, dtype=jnp.int32)[None, :]
    a = order[jnp.minimum(row0[:, None] + r, TOP_K * t - 1)]
    row_token = jnp.where(r < n_rows[:, None], jnp.where(a >= t, a - t, a), 0).reshape(-1)
    row_of = jnp.sum((csum - 1 + (tile_begin * tm)[None, :]) * onehot, axis=1).astype(jnp.int32)
    n_valid = tile_end[-1].astype(jnp.int32).reshape(1)
    return top_w, row_token.astype(jnp.int32), row_of, tile_expert.astype(jnp.int32), n_valid, n_tiles


def _invert_body(pos_ref, order_ref, *, n):
    def body(a, c):
        order_ref[pos_ref[a]] = a
        return c
    lax.fori_loop(0, n, body, 0, unroll=16)


def _invert_permutation(pos):
    n = pos.shape[0]
    return pl.pallas_call(
        functools.partial(_invert_body, n=n),
        in_specs=[pl.BlockSpec(memory_space=pltpu.SMEM)],
        out_specs=pl.BlockSpec(memory_space=pltpu.SMEM),
        out_shape=jax.ShapeDtypeStruct((n,), jnp.int32),
        name="moe_invert",
    )(pos)


def _row_tile(r):
    return pl.ds(pl.multiple_of(r * ROW_CHUNKS, ROW_CHUNKS), ROW_CHUNKS)


def _row_dma_loop(n_rows, issue):
    def body(g, c):
        for j in range(8):
            issue(g * 8 + j, j % 2)
        return c
    lax.fori_loop(0, n_rows // 8, body, 0)


def _expert_body(te_ref, nv_ref, tok_ref, h_ref, wg_ref, wu_ref, wd_ref, y_ref, xbuf, gsem,
                 *, tm):
    del te_ref
    i = pl.program_id(0)
    n_valid = nv_ref[0]
    slot = i % 2
    rc = ROW_CHUNKS

    def gather_start(tile, sl):
        def issue(r, queue):
            pltpu.make_async_copy(h_ref.at[_row_tile(tok_ref[tile * tm + r])],
                                  xbuf.at[sl, _row_tile(r)], gsem.at[sl]).start(priority=queue)
        _row_dma_loop(tm, issue)

    @pl.when(jnp.logical_and(i == 0, n_valid > 0))
    def _():
        gather_start(0, 0)

    @pl.when(i + 1 < n_valid)
    def _():
        gather_start(i + 1, 1 - slot)

    @pl.when(i < n_valid)
    def _():
        pltpu.make_async_copy(h_ref.at[pl.ds(0, tm * rc)], xbuf.at[slot], gsem.at[slot]).wait()
        x = jnp.concatenate(
            [xbuf[slot, pl.ds(c, tm, stride=rc), :].astype(BF16) for c in range(rc)], axis=1)
        act = (jax.nn.silu(_dot(x, wg_ref[0])) * _dot(x, wu_ref[0])).astype(BF16)
        y = _dot(act, wd_ref[0])
        for c in range(rc):
            y_ref[pl.ds(c, tm, stride=rc), :] = y[:, c * LANES:(c + 1) * LANES]

    @pl.when(i >= n_valid)
    def _():
        y_ref[...] = jnp.zeros_like(y_ref)


def _experts(tile_expert, n_valid, row_token, h, wg, wu, wd, tm, n_tiles):
    dfe = wg.shape[2]
    wmap = lambda i, te, nv, tk: (te[i], 0, 0)
    return pl.pallas_call(
        functools.partial(_expert_body, tm=tm),
        grid_spec=pltpu.PrefetchScalarGridSpec(
            num_scalar_prefetch=3,
            grid=(n_tiles,),
            in_specs=[pl.BlockSpec(memory_space=pl.ANY),
                      pl.BlockSpec((1, D_MODEL, dfe), wmap),
                      pl.BlockSpec((1, D_MODEL, dfe), wmap),
                      pl.BlockSpec((1, dfe, D_MODEL), wmap)],
            out_specs=pl.BlockSpec((tm * ROW_CHUNKS, LANES), lambda i, te, nv, tk: (i, 0)),
            scratch_shapes=[pltpu.VMEM((2, tm * ROW_CHUNKS, LANES), F32),
                            pltpu.SemaphoreType.DMA((2,))],
        ),
        out_shape=jax.ShapeDtypeStruct((n_tiles * tm * ROW_CHUNKS, LANES), F32),
        compiler_params=_params(1),
        name="moe_experts",
    )(tile_expert, n_valid, row_token, h, wg, wu, wd)


def _combine_body(row_ref, x_ref, w_ref, g_ref, y_ref, o_ref, ybuf, sem,
                  *, tm, t, n_steps, final_norm):
    i = pl.program_id(0)
    slot = i % 2
    rc = ROW_CHUNKS

    def gather_start(step, sl):
        def issue(r, queue):
            for k in range(TOP_K):
                pltpu.make_async_copy(y_ref.at[_row_tile(row_ref[k * t + step * tm + r])],
                                      ybuf.at[sl, _row_tile(k * tm + r)],
                                      sem.at[sl]).start(priority=(queue + k) % 2)
        _row_dma_loop(tm, issue)

    @pl.when(i == 0)
    def _():
        gather_start(0, 0)

    @pl.when(i + 1 < n_steps)
    def _():
        gather_start(i + 1, 1 - slot)

    pltpu.make_async_copy(y_ref.at[pl.ds(0, TOP_K * tm * rc)], ybuf.at[slot], sem.at[slot]).wait()
    rows = lambda k: jnp.concatenate(
        [ybuf[slot, pl.ds(k * tm * rc + c, tm, stride=rc), :] for c in range(rc)], axis=1)
    w = w_ref[...]
    xn = x_ref[...] + (w[:, 0:1] * rows(0) + w[:, 1:2] * rows(1))
    o_ref[...] = _rms(xn, g_ref[...]) if final_norm else xn


def _combine(row_of, x2, y, top_w, g, final_norm):
    t = x2.shape[0]
    tm = min(512, t)
    n_steps = t // tm
    return pl.pallas_call(
        functools.partial(_combine_body, tm=tm, t=t, n_steps=n_steps, final_norm=final_norm),
        grid_spec=pltpu.PrefetchScalarGridSpec(
            num_scalar_prefetch=1,
            grid=(n_steps,),
            in_specs=[pl.BlockSpec((tm, D_MODEL), lambda i, ro: (i, 0)),
                      pl.BlockSpec((tm, TOP_K), lambda i, ro: (i, 0)),
                      pl.BlockSpec((1, D_MODEL), lambda i, ro: (0, 0)),
                      pl.BlockSpec(memory_space=pl.ANY)],
            out_specs=pl.BlockSpec((tm, D_MODEL), lambda i, ro: (i, 0)),
            scratch_shapes=[pltpu.VMEM((2, TOP_K * tm * ROW_CHUNKS, LANES), F32),
                            pltpu.SemaphoreType.DMA((2,))],
        ),
        out_shape=jax.ShapeDtypeStruct((t, D_MODEL), F32),
        compiler_params=_params(1),
        name="moe_combine",
    )(row_of, x2, top_w, g, y)


def _final_norm_body(x_ref, g_ref, o_ref):
    o_ref[...] = _rms(x_ref[...], g_ref[...])


def _final_norm(x2, g):
    t = x2.shape[0]
    tm = min(512, t)
    return pl.pallas_call(
        _final_norm_body,
        grid=(t // tm,),
        in_specs=[pl.BlockSpec((tm, D_MODEL), lambda i: (i, 0)),
                  pl.BlockSpec((1, D_MODEL), lambda i: (0, 0))],
        out_specs=pl.BlockSpec((tm, D_MODEL), lambda i: (i, 0)),
        out_shape=jax.ShapeDtypeStruct((t, D_MODEL), F32),
        compiler_params=_params(1),
        name="final_norm",
    )(x2, g)


def _router_weights(w):
    wp = jnp.pad(w, ((0, 0), (0, ROUTER_PAD - N_EXPERTS)))
    hi = wp.astype(BF16)
    lo = (wp - hi.astype(F32)).astype(BF16)
    return jnp.concatenate([hi, lo], axis=1)


def kernel(x, g_mix, w_in, conv_w, w_branch_conv, w_branch_attn, w_out, g_ffn, w_ffn_gate, w_ffn_up, w_ffn_down, w_router, w_exp_gate, w_exp_up, w_exp_down, g_final):
    b, s, d = x.shape
    depth = g_mix.shape[0]
    t = b * s
    x2 = x.reshape(t, d)
    gf = g_final.reshape(1, d)
    for i in range(depth):
        last = i == depth - 1
        moe = i % 2 == 1
        j = i // 2
        qz, k, vz, ma, gbs = _mix_in(x2, g_mix[i].reshape(1, d), w_in[i].astype(BF16), conv_w[i],
                                     w_branch_conv[i].astype(BF16), s)
        ub = _attention(qz.reshape(b, s, D_SPLIT), k.reshape(b, s, D_ATTN),
                        vz.reshape(b, s, D_SPLIT)).reshape(t, D_ATTN)
        wb = w_branch_attn[i].astype(BF16)
        wo = w_out[i].astype(BF16)
        gi = g_ffn[i].reshape(1, d)
        if not moe:
            x2 = _mix_out_dense(x2, ma, gbs, ub, wb, wo, gi, w_ffn_gate[j].astype(BF16),
                                w_ffn_up[j].astype(BF16), w_ffn_down[j].astype(BF16))
            if last:
                x2 = _final_norm(x2, gf)
        else:
            x2, h, logits = _mix_out_router(x2, ma, gbs, ub, wb, wo, gi,
                                            _router_weights(w_router[j]))
            tm = min(512, t)
            top_w, row_token, row_of, tile_expert, n_valid, n_tiles = _route(
                logits[:, :N_EXPERTS], tm)
            y = _experts(tile_expert, n_valid, row_token, h,
                         w_exp_gate[j].astype(BF16), w_exp_up[j].astype(BF16),
                         w_exp_down[j].astype(BF16), tm, n_tiles)
            x2 = _combine(row_of, x2, y, top_w, gf, final_norm=last)
    return x2.reshape(b, s, d)
```

```python
import functools
import math

import jax
import jax.numpy as jnp
from jax import lax
from jax.experimental import pallas as pl
from jax.experimental.pallas import tpu as pltpu

F32 = jnp.float32
BF16 = jnp.bfloat16

D_MODEL = 1024
D_CONV = 512
D_ATTN = 512
N_HEADS = 8
HEAD_DIM = 64
N_EXPERTS = 8
TOP_K = 2
EPS = 1e-6
CONV_WIDTH = 3

LANES = 128
VMEM_LIMIT_BYTES = 56 * 1024 * 1024

ATTN_BLOCK = 128
ATTN_GROUP = 4
ATTN_FUSED_STEPS = 3
N_PAIRS = N_HEADS // 2
D_SPLIT = 2 * D_ATTN
LOG2_F32_ZERO = -151.0
ROUTER_PAD = 128
ROW_CHUNKS = D_MODEL // LANES


def _params(n_grid, vmem=VMEM_LIMIT_BYTES):
    return pltpu.CompilerParams(
        dimension_semantics=("arbitrary",) * n_grid, vmem_limit_bytes=vmem)


def _rms(x, g):
    return x * lax.rsqrt(jnp.mean(x * x, axis=-1, keepdims=True) + EPS) * g


def _dot(a, b):
    return jnp.dot(a, b, preferred_element_type=F32)


def _const_spec(shape):
    return pl.BlockSpec(shape, lambda i: (0,) * len(shape), pipeline_mode=pl.Buffered(1))


def _mix_in_body(x_ref, g_ref, win_ref, cw_ref, wa_ref,
                 qz_ref, k_ref, vz_ref, ma_ref, gbs_ref, ubuf, *, tm, tiles_per_seq):
    i = pl.program_id(0)
    h = _rms(x_ref[...], g_ref[...]).astype(BF16)

    p = _dot(h, win_ref[:, 0:3 * D_CONV])
    u = p[:, 2 * D_CONV:3 * D_CONV] * p[:, 0:D_CONV]

    @pl.when(i % tiles_per_seq == 0)
    def _():
        ubuf[0:8, :] = jnp.zeros((8, D_CONV), F32)

    ubuf[8:8 + tm, :] = u
    cw = cw_ref[...]
    conv = (cw[0:1, :] * ubuf[6:6 + tm, :] + cw[1:2, :] * ubuf[7:7 + tm, :]) + cw[2:3, :] * u
    ubuf[0:8, :] = ubuf[tm:tm + 8, :]
    ua = (p[:, D_CONV:2 * D_CONV] * conv).astype(BF16)

    o = 3 * D_CONV
    qkv = _dot(h, win_ref[:, o:o + 3 * D_ATTN])
    o += 3 * D_ATTN
    ga = _dot(h, win_ref[:, o:o + D_MODEL])
    gb = _dot(h, win_ref[:, o + D_MODEL:o + 2 * D_MODEL])

    first = lax.broadcasted_iota(jnp.int32, (tm, LANES), 1) < HEAD_DIM
    q_scale = (HEAD_DIM ** -0.5) * math.log2(math.e)
    for pr in range(N_PAIRS):
        qs = qkv[:, pr * LANES:(pr + 1) * LANES] * q_scale
        vs = qkv[:, 2 * D_ATTN + pr * LANES:2 * D_ATTN + (pr + 1) * LANES]
        lo, mid, hi = 2 * pr * LANES, (2 * pr + 1) * LANES, (2 * pr + 2) * LANES
        qz_ref[:, lo:mid] = jnp.where(first, qs, 0.0).astype(BF16)
        qz_ref[:, mid:hi] = jnp.where(first, 0.0, qs).astype(BF16)
        vz_ref[:, lo:mid] = jnp.where(first, vs, 0.0).astype(BF16)
        vz_ref[:, mid:hi] = jnp.where(first, 0.0, vs).astype(BF16)
    k_ref[...] = qkv[:, D_ATTN:2 * D_ATTN].astype(BF16)
    gbs_ref[...] = jax.nn.sigmoid(gb).astype(BF16)
    ma_ref[...] = (jax.nn.sigmoid(ga) * _dot(ua, wa_ref[...])).astype(BF16)


def _mix_in(x2, g, win, cw, wa, seq):
    t = x2.shape[0]
    tm = min(512, seq)
    n_in = win.shape[1]
    body = functools.partial(_mix_in_body, tm=tm, tiles_per_seq=seq // tm)
    row = lambda w: pl.BlockSpec((tm, w), lambda i: (i, 0))
    return pl.pallas_call(
        body,
        grid=(t // tm,),
        in_specs=[row(D_MODEL), _const_spec((1, D_MODEL)), _const_spec((D_MODEL, n_in)),
                  _const_spec((CONV_WIDTH, D_CONV)), _const_spec((D_CONV, D_MODEL))],
        out_specs=[row(D_SPLIT), row(D_ATTN), row(D_SPLIT), row(D_MODEL), row(D_MODEL)],
        out_shape=[jax.ShapeDtypeStruct((t, D_SPLIT), BF16),
                   jax.ShapeDtypeStruct((t, D_ATTN), BF16),
                   jax.ShapeDtypeStruct((t, D_SPLIT), BF16),
                   jax.ShapeDtypeStruct((t, D_MODEL), BF16),
                   jax.ShapeDtypeStruct((t, D_MODEL), BF16)],
        scratch_shapes=[pltpu.VMEM((tm + 8, D_CONV), F32)],
        compiler_params=_params(1),
        name="mix_in",
    )(x2, g, win, cw, wa)


def _attn_body(qz_ref, k_ref, vz_ref, o_ref, r_scr, o_scr, m_scr):
    i = pl.program_id(1)
    tb = ATTN_BLOCK
    row2 = lax.broadcasted_iota(jnp.int32, (2 * tb, tb), 0)
    col2 = lax.broadcasted_iota(jnp.int32, (2 * tb, tb), 1)
    causal2 = col2 < jnp.where(row2 >= tb, row2 - tb, row2)
    rj = lax.broadcasted_iota(jnp.int32, (tb, 2 * tb), 0)
    cs_ = lax.broadcasted_iota(jnp.int32, (tb, 2 * tb), 1)
    suffix = jnp.where(jnp.logical_or(rj > cs_, cs_ >= tb), 1.0, 0.0).astype(BF16)

    def sweep(steps, groups, fresh):
        chains = [(g, p) for g in groups for p in range(N_PAIRS)]
        start = {(g, s): pl.multiple_of((ATTN_GROUP * i + g - n) * tb, tb)
                 for g in groups for s, n in enumerate(steps)}
        diag = [isinstance(n, int) and n == 0 for n in steps]
        zs, log_betas, log_1ms, cs = {}, {}, {}, {}
        rs = {ci: (None if fresh else r_scr[g * N_PAIRS + p]) for ci, (g, p) in enumerate(chains)}
        os_ = {ci: (None if fresh else o_scr[g * N_PAIRS + p]) for ci, (g, p) in enumerate(chains)}

        def scores(s):
            for ci, (g, p) in enumerate(chains):
                rows = slice(g * tb, (g + 1) * tb)
                q2 = jnp.concatenate(
                    [qz_ref[0, rows, 2 * p * LANES:(2 * p + 1) * LANES],
                     qz_ref[0, rows, (2 * p + 1) * LANES:(2 * p + 2) * LANES]], axis=0)
                kp = k_ref[0, pl.ds(start[g, s], tb), p * LANES:(p + 1) * LANES]
                zs[s, ci] = lax.dot_general(q2, kp, (((1,), (1,)), ((), ())),
                                            preferred_element_type=F32)

        def logs_and_suffix_sums(s):
            parts = []
            for ci in range(len(chains)):
                z = zs[s, ci]
                sp = jnp.log2(1.0 + jnp.exp2(-jnp.abs(z)))
                log_beta = jnp.minimum(z, 0.0) - sp
                log_1m = log_beta - z
                if diag[s]:
                    log_1m = jnp.where(causal2, log_1m, 0.0)
                log_betas[s, ci] = log_beta
                parts.append(log_1m.astype(BF16))
            cs[s] = _dot(jnp.concatenate(parts, axis=0), suffix)

        def weights_and_values(s):
            for ci, (g, p) in enumerate(chains):
                c = cs[s][ci * 2 * tb:(ci + 1) * 2 * tb]
                r = rs[ci]
                t = log_betas[s, ci] + c[:, 0:tb]
                a = jnp.exp2(t if r is None else t + r)
                if diag[s]:
                    a = jnp.where(causal2, a, 0.0)
                rs[ci] = c[:, tb:2 * tb] if r is None else r + c[:, tb:2 * tb]
                a2 = jnp.concatenate([a[0:tb], a[tb:2 * tb]], axis=1).astype(BF16)
                v2 = jnp.concatenate(
                    [vz_ref[0, pl.ds(start[g, s], tb), 2 * p * LANES:(2 * p + 1) * LANES],
                     vz_ref[0, pl.ds(start[g, s], tb), (2 * p + 1) * LANES:(2 * p + 2) * LANES]],
                    axis=0)
                pv = _dot(a2, v2)
                os_[ci] = pv if os_[ci] is None else os_[ci] + pv

        scores(0)
        for s in range(len(steps)):
            if s + 1 < len(steps):
                scores(s + 1)
            logs_and_suffix_sums(s)
            if s >= 1:
                weights_and_values(s - 1)
        weights_and_values(len(steps) - 1)

        rmax = None
        for ci, (g, p) in enumerate(chains):
            r_scr[g * N_PAIRS + p] = rs[ci]
            o_scr[g * N_PAIRS + p] = os_[ci]
            rmax = rs[ci] if rmax is None else jnp.maximum(rmax, rs[ci])
        return jnp.max(rmax, axis=0, keepdims=True)[0, 0]

    all_groups = list(range(ATTN_GROUP))
    fused = list(range(min(ATTN_FUSED_STEPS, ATTN_GROUP + 1)))

    def tails(first_extra):
        for extra in range(first_extra, ATTN_GROUP):
            sweep([ATTN_GROUP * i + extra], all_groups[extra:], False)

    @pl.when(i == 0)
    def _():
        m_scr[0] = sweep([0], all_groups, True)

        @pl.when(m_scr[0] > LOG2_F32_ZERO)
        def _():
            tails(1)

    @pl.when(i > 0)
    def _():
        m_scr[0] = sweep(fused, all_groups, True)

    def cond(c):
        n, m = c
        return jnp.logical_and(n <= ATTN_GROUP * i, m > LOG2_F32_ZERO)

    def body(c):
        n, _ = c
        return n + 1, sweep([n], all_groups, False)

    n_end, m_end = lax.while_loop(cond, body, (jnp.int32(len(fused)), m_scr[0]))

    @pl.when(jnp.logical_and(i > 0, jnp.logical_and(n_end == ATTN_GROUP * i + 1,
                                                     m_end > LOG2_F32_ZERO)))
    def _():
        tails(1)

    for g in range(ATTN_GROUP):
        for p in range(N_PAIRS):
            o_ref[0, g * tb:(g + 1) * tb, p * LANES:(p + 1) * LANES] = (
                o_scr[g * N_PAIRS + p].astype(o_ref.dtype))


def _attention(qz, k, vz):
    b, s, _ = k.shape
    tb = ATTN_BLOCK
    tq = ATTN_GROUP * tb
    n_chains = ATTN_GROUP * N_PAIRS
    return pl.pallas_call(
        _attn_body,
        grid=(b, s // tq),
        in_specs=[pl.BlockSpec((1, tq, D_SPLIT), lambda bi, i: (bi, i, 0)),
                  pl.BlockSpec((1, s, D_ATTN), lambda bi, i: (bi, 0, 0)),
                  pl.BlockSpec((1, s, D_SPLIT), lambda bi, i: (bi, 0, 0))],
        out_specs=pl.BlockSpec((1, tq, D_ATTN), lambda bi, i: (bi, i, 0)),
        out_shape=jax.ShapeDtypeStruct((b, s, D_ATTN), BF16),
        scratch_shapes=[pltpu.VMEM((n_chains, 2 * tb, tb), F32),
                        pltpu.VMEM((n_chains, tb, LANES), F32),
                        pltpu.SMEM((1,), F32)],
        compiler_params=_params(2),
        name="stickbreak_attn",
    )(qz, k, vz)


def _merge(x_ref, ma_ref, gbs_ref, ub_ref, wb_ref, wo_ref, g_ref):
    mix = ma_ref[...].astype(F32) + gbs_ref[...].astype(F32) * _dot(ub_ref[...], wb_ref[...])
    xn = x_ref[...] + _dot(mix.astype(BF16), wo_ref[...])
    return xn, _rms(xn, g_ref[...])


def _mix_out_dense_body(x_ref, ma_ref, gbs_ref, ub_ref, wb_ref, wo_ref, g_ref,
                        wg_ref, wu_ref, wd_ref, o_ref):
    xn, h = _merge(x_ref, ma_ref, gbs_ref, ub_ref, wb_ref, wo_ref, g_ref)
    h = h.astype(BF16)
    act = (jax.nn.silu(_dot(h, wg_ref[...])) * _dot(h, wu_ref[...])).astype(BF16)
    o_ref[...] = xn + _dot(act, wd_ref[...])


def _mix_out_router_body(x_ref, ma_ref, gbs_ref, ub_ref, wb_ref, wo_ref, g_ref, wr_ref,
                         xo_ref, h_ref, lg_ref, *, tm):
    xn, h = _merge(x_ref, ma_ref, gbs_ref, ub_ref, wb_ref, wo_ref, g_ref)
    xo_ref[...] = xn
    for c in range(ROW_CHUNKS):
        h_ref[pl.ds(c, tm, stride=ROW_CHUNKS), :] = h[:, c * LANES:(c + 1) * LANES]
    h_hi = h.astype(BF16)
    h_lo = (h - h_hi.astype(F32)).astype(BF16)
    c2 = _dot(h_hi, wr_ref[...])
    lg_ref[...] = (c2[:, 0:ROUTER_PAD] + c2[:, ROUTER_PAD:2 * ROUTER_PAD]
                   + _dot(h_lo, wr_ref[:, 0:ROUTER_PAD]))


def _mix_out_specs(tm):
    row = lambda w: pl.BlockSpec((tm, w), lambda i: (i, 0))
    return [row(D_MODEL), row(D_MODEL), row(D_MODEL), row(D_ATTN),
            _const_spec((D_ATTN, D_MODEL)), _const_spec((D_MODEL, D_MODEL)),
            _const_spec((1, D_MODEL))]


def _mix_out_dense(x2, ma, gbs, ub, wb, wo, g, wg, wu, wd):
    t = x2.shape[0]
    tm = min(512, t)
    dff = wg.shape[1]
    return pl.pallas_call(
        _mix_out_dense_body,
        grid=(t // tm,),
        in_specs=_mix_out_specs(tm) + [_const_spec((D_MODEL, dff)), _const_spec((D_MODEL, dff)),
                                       _const_spec((dff, D_MODEL))],
        out_specs=pl.BlockSpec((tm, D_MODEL), lambda i: (i, 0)),
        out_shape=jax.ShapeDtypeStruct((t, D_MODEL), F32),
        compiler_params=_params(1),
        name="mix_out_dense_ffn",
    )(x2, ma, gbs, ub, wb, wo, g, wg, wu, wd)


def _mix_out_router(x2, ma, gbs, ub, wb, wo, g, wr):
    t = x2.shape[0]
    tm = min(512, t)
    row = lambda w: pl.BlockSpec((tm, w), lambda i: (i, 0))
    return pl.pallas_call(
        functools.partial(_mix_out_router_body, tm=tm),
        grid=(t // tm,),
        in_specs=_mix_out_specs(tm) + [_const_spec((D_MODEL, 2 * ROUTER_PAD))],
        out_specs=[row(D_MODEL), pl.BlockSpec((tm * ROW_CHUNKS, LANES), lambda i: (i, 0)),
                   row(ROUTER_PAD)],
        out_shape=[jax.ShapeDtypeStruct((t, D_MODEL), F32),
                   jax.ShapeDtypeStruct((t * ROW_CHUNKS, LANES), F32),
                   jax.ShapeDtypeStruct((t, ROUTER_PAD), F32)],
        compiler_params=_params(1),
        name="mix_out_router",
    )(x2, ma, gbs, ub, wb, wo, g, wr)


def _route(logits, tm):
    t = logits.shape[0]
    top_vals, top_idx = lax.top_k(logits, TOP_K)
    e_flat = top_idx.reshape(-1).astype(jnp.int32)
    experts = jnp.arange(N_EXPERTS, dtype=jnp.int32)
    onehot = (e_flat[:, None] == experts[None, :]).astype(jnp.int32)
    csum = jnp.cumsum(onehot, axis=0)
    counts = csum[-1]
    starts = jnp.cumsum(counts) - counts
    pos = jnp.sum((csum - 1 + starts[None, :]) * onehot, axis=1).astype(jnp.int32)
    order = _invert_permutation(pos)
    tiles_e = (counts + tm - 1) // tm
    tile_end = jnp.cumsum(tiles_e)
    tile_begin = tile_end - tiles_e
    n_tiles = (TOP_K * t) // tm + N_EXPERTS
    ti = jnp.arange(n_tiles, dtype=jnp.int32)
    tile_expert = jnp.minimum(
        jnp.sum((ti[:, None] >= tile_end[None, :]).astype(jnp.int32), axis=1), N_EXPERTS - 1)
    local = ti - tile_begin[tile_expert]
    row0 = starts[tile_expert] + local * tm
    n_rows = jnp.clip(counts[tile_expert] - local * tm, 0, tm)
    r = jnp.arange(tm, dtype=jnp.int32)[None, :]
    a = order[jnp.minimum(row0[:, None] + r, TOP_K * t - 1)]
    row_token = jnp.where(r < n_rows[:, None], a // TOP_K, 0).reshape(-1)
    row_of = jnp.sum((csum - 1 + (tile_begin * tm)[None, :]) * onehot, axis=1).astype(jnp.int32)
    n_valid = tile_end[-1].astype(jnp.int32).reshape(1)
    return top_vals, row_token.astype(jnp.int32), row_of, tile_expert.astype(jnp.int32), n_valid, n_tiles


def _invert_body(pos_ref, order_ref, *, n):
    def body(a, c):
        order_ref[pos_ref[a]] = a
        return c
    lax.fori_loop(0, n, body, 0, unroll=16)


def _invert_permutation(pos):
    n = pos.shape[0]
    return pl.pallas_call(
        functools.partial(_invert_body, n=n),
        in_specs=[pl.BlockSpec(memory_space=pltpu.SMEM)],
        out_specs=pl.BlockSpec(memory_space=pltpu.SMEM),
        out_shape=jax.ShapeDtypeStruct((n,), jnp.int32),
        name="moe_invert",
    )(pos)


def _row_tile(r):
    return pl.ds(pl.multiple_of(r * ROW_CHUNKS, ROW_CHUNKS), ROW_CHUNKS)


ROW_DMA_UNROLL = 32


def _row_dma_loop(n_rows, issue):
    def body(g, c):
        for j in range(ROW_DMA_UNROLL):
            issue(g * ROW_DMA_UNROLL + j, j % 2)
        return c
    lax.fori_loop(0, n_rows // ROW_DMA_UNROLL, body, 0)


def _expert_body(te_ref, nv_ref, tok_ref, h_ref, wg_ref, wu_ref, wd_ref, y_ref, xbuf, gsem,
                 *, tm):
    del te_ref
    i = pl.program_id(0)
    n_valid = nv_ref[0]
    slot = i % 2
    rc = ROW_CHUNKS

    def gather_start(tile, sl):
        def issue(r, queue):
            pltpu.make_async_copy(h_ref.at[_row_tile(tok_ref[tile * tm + r])],
                                  xbuf.at[sl, _row_tile(r)], gsem.at[sl]).start(priority=queue)
        _row_dma_loop(tm, issue)

    @pl.when(jnp.logical_and(i == 0, n_valid > 0))
    def _():
        gather_start(0, 0)

    @pl.when(i + 1 < n_valid)
    def _():
        gather_start(i + 1, 1 - slot)

    @pl.when(i < n_valid)
    def _():
        pltpu.make_async_copy(h_ref.at[pl.ds(0, tm * rc)], xbuf.at[slot], gsem.at[slot]).wait()
        x = jnp.concatenate(
            [xbuf[slot, pl.ds(c, tm, stride=rc), :].astype(BF16) for c in range(rc)], axis=1)
        act = (jax.nn.silu(_dot(x, wg_ref[0])) * _dot(x, wu_ref[0])).astype(BF16)
        y = _dot(act, wd_ref[0])
        for c in range(rc):
            y_ref[pl.ds(c, tm, stride=rc), :] = y[:, c * LANES:(c + 1) * LANES]

    @pl.when(i >= n_valid)
    def _():
        y_ref[...] = jnp.zeros_like(y_ref)


def _experts(tile_expert, n_valid, row_token, h, wg, wu, wd, tm, n_tiles):
    dfe = wg.shape[2]
    wmap = lambda i, te, nv, tk: (te[i], 0, 0)
    return pl.pallas_call(
        functools.partial(_expert_body, tm=tm),
        grid_spec=pltpu.PrefetchScalarGridSpec(
            num_scalar_prefetch=3,
            grid=(n_tiles,),
            in_specs=[pl.BlockSpec(memory_space=pl.ANY),
                      pl.BlockSpec((1, D_MODEL, dfe), wmap),
                      pl.BlockSpec((1, D_MODEL, dfe), wmap),
                      pl.BlockSpec((1, dfe, D_MODEL), wmap)],
            out_specs=pl.BlockSpec((tm * ROW_CHUNKS, LANES), lambda i, te, nv, tk: (i, 0)),
            scratch_shapes=[pltpu.VMEM((2, tm * ROW_CHUNKS, LANES), F32),
                            pltpu.SemaphoreType.DMA((2,))],
        ),
        out_shape=jax.ShapeDtypeStruct((n_tiles * tm * ROW_CHUNKS, LANES), F32),
        compiler_params=_params(1),
        name="moe_experts",
    )(tile_expert, n_valid, row_token, h, wg, wu, wd)


def _combine_body(row_ref, x_ref, v_ref, g_ref, y_ref, o_ref, ybuf, sem,
                  *, tm, n_steps, final_norm):
    i = pl.program_id(0)
    slot = i % 2
    rc = ROW_CHUNKS

    def gather_start(step, sl):
        def issue(r, queue):
            for k in range(TOP_K):
                pltpu.make_async_copy(y_ref.at[_row_tile(row_ref[TOP_K * (step * tm + r) + k])],
                                      ybuf.at[sl, _row_tile(k * tm + r)],
                                      sem.at[sl]).start(priority=(queue + k) % 2)
        _row_dma_loop(tm, issue)

    @pl.when(i == 0)
    def _():
        gather_start(0, 0)

    @pl.when(i + 1 < n_steps)
    def _():
        gather_start(i + 1, 1 - slot)

    pltpu.make_async_copy(y_ref.at[pl.ds(0, TOP_K * tm * rc)], ybuf.at[slot], sem.at[slot]).wait()
    rows = lambda k: jnp.concatenate(
        [ybuf[slot, pl.ds(k * tm * rc + c, tm, stride=rc), :] for c in range(rc)], axis=1)
    v = v_ref[...]
    e = jnp.exp(v - jnp.max(v, axis=-1, keepdims=True))
    w = e / jnp.sum(e, axis=-1, keepdims=True)
    xn = x_ref[...] + (w[:, 0:1] * rows(0) + w[:, 1:2] * rows(1))
    o_ref[...] = _rms(xn, g_ref[...]) if final_norm else xn


def _combine(row_of, x2, y, top_vals, g, final_norm):
    t = x2.shape[0]
    tm = min(512, t)
    n_steps = t // tm
    return pl.pallas_call(
        functools.partial(_combine_body, tm=tm, n_steps=n_steps, final_norm=final_norm),
        grid_spec=pltpu.PrefetchScalarGridSpec(
            num_scalar_prefetch=1,
            grid=(n_steps,),
            in_specs=[pl.BlockSpec((tm, D_MODEL), lambda i, ro: (i, 0)),
                      pl.BlockSpec((tm, TOP_K), lambda i, ro: (i, 0)),
                      pl.BlockSpec((1, D_MODEL), lambda i, ro: (0, 0)),
                      pl.BlockSpec(memory_space=pl.ANY)],
            out_specs=pl.BlockSpec((tm, D_MODEL), lambda i, ro: (i, 0)),
            scratch_shapes=[pltpu.VMEM((2, TOP_K * tm * ROW_CHUNKS, LANES), F32),
                            pltpu.SemaphoreType.DMA((2,))],
        ),
        out_shape=jax.ShapeDtypeStruct((t, D_MODEL), F32),
        compiler_params=_params(1),
        name="moe_combine",
    )(row_of, x2, top_vals, g, y)


def _final_norm_body(x_ref, g_ref, o_ref):
    o_ref[...] = _rms(x_ref[...], g_ref[...])


def _final_norm(x2, g):
    t = x2.shape[0]
    tm = min(512, t)
    return pl.pallas_call(
        _final_norm_body,
        grid=(t // tm,),
        in_specs=[pl.BlockSpec((tm, D_MODEL), lambda i: (i, 0)),
                  pl.BlockSpec((1, D_MODEL), lambda i: (0, 0))],
        out_specs=pl.BlockSpec((tm, D_MODEL), lambda i: (i, 0)),
        out_shape=jax.ShapeDtypeStruct((t, D_MODEL), F32),
        compiler_params=_params(1),
        name="final_norm",
    )(x2, g)


def _router_weights(w):
    wp = jnp.pad(w, ((0, 0), (0, ROUTER_PAD - N_EXPERTS)))
    hi = wp.astype(BF16)
    lo = (wp - hi.astype(F32)).astype(BF16)
    return jnp.concatenate([hi, lo], axis=1)


def kernel(x, g_mix, w_in, conv_w, w_branch_conv, w_branch_attn, w_out, g_ffn, w_ffn_gate, w_ffn_up, w_ffn_down, w_router, w_exp_gate, w_exp_up, w_exp_down, g_final):
    b, s, d = x.shape
    depth = g_mix.shape[0]
    t = b * s
    x2 = x.reshape(t, d)
    gf = g_final.reshape(1, d)
    for i in range(depth):
        last = i == depth - 1
        moe = i % 2 == 1
        j = i // 2
        qz, k, vz, ma, gbs = _mix_in(x2, g_mix[i].reshape(1, d), w_in[i].astype(BF16), conv_w[i],
                                     w_branch_conv[i].astype(BF16), s)
        ub = _attention(qz.reshape(b, s, D_SPLIT), k.reshape(b, s, D_ATTN),
                        vz.reshape(b, s, D_SPLIT)).reshape(t, D_ATTN)
        wb = w_branch_attn[i].astype(BF16)
        wo = w_out[i].astype(BF16)
        gi = g_ffn[i].reshape(1, d)
        if not moe:
            x2 = _mix_out_dense(x2, ma, gbs, ub, wb, wo, gi, w_ffn_gate[j].astype(BF16),
                                w_ffn_up[j].astype(BF16), w_ffn_down[j].astype(BF16))
            if last:
                x2 = _final_norm(x2, gf)
        else:
            x2, h, logits = _mix_out_router(x2, ma, gbs, ub, wb, wo, gi,
                                            _router_weights(w_router[j]))
            tm = min(512, t)
            top_vals, row_token, row_of, tile_expert, n_valid, n_tiles = _route(
                logits[:, :N_EXPERTS], tm)
            y = _experts(tile_expert, n_valid, row_token, h,
                         w_exp_gate[j].astype(BF16), w_exp_up[j].astype(BF16),
                         w_exp_down[j].astype(BF16), tm, n_tiles)
            x2 = _combine(row_of, x2, y, top_vals, gf, final_norm=last)
    return x2.reshape(b, s, d)
```

```python
import functools
import math

import jax
import jax.numpy as jnp
from jax import lax
from jax.experimental import pallas as pl
from jax.experimental.pallas import tpu as pltpu

F32 = jnp.float32
BF16 = jnp.bfloat16

D_MODEL = 1024
D_CONV = 512
D_ATTN = 512
N_HEADS = 8
HEAD_DIM = 64
N_EXPERTS = 8
TOP_K = 2
EPS = 1e-6
CONV_WIDTH = 3

LANES = 128
VMEM_LIMIT_BYTES = 56 * 1024 * 1024

ATTN_BLOCK = 128
ATTN_GROUP = 4
ATTN_FUSED_STEPS = 3
N_PAIRS = N_HEADS // 2
D_SPLIT = 2 * D_ATTN
LOG2_F32_ZERO = -151.0
ROUTER_PAD = 128
ROW_CHUNKS = D_MODEL // LANES


def _params(n_grid, vmem=VMEM_LIMIT_BYTES):
    return pltpu.CompilerParams(
        dimension_semantics=("arbitrary",) * n_grid, vmem_limit_bytes=vmem)


def _rms(x, g):
    return x * lax.rsqrt(jnp.mean(x * x, axis=-1, keepdims=True) + EPS) * g


def _dot(a, b):
    return jnp.dot(a, b, preferred_element_type=F32)


def _const_spec(shape):
    return pl.BlockSpec(shape, lambda i: (0,) * len(shape), pipeline_mode=pl.Buffered(1))


def _mix_in_body(x_ref, g_ref, win_ref, cw_ref, wa_ref,
                 qz_ref, k_ref, vz_ref, ma_ref, gbs_ref, ubuf, *, tm, tiles_per_seq):
    i = pl.program_id(0)
    h = _rms(x_ref[...], g_ref[...]).astype(BF16)

    p = _dot(h, win_ref[:, 0:3 * D_CONV])
    u = p[:, 2 * D_CONV:3 * D_CONV] * p[:, 0:D_CONV]

    @pl.when(i % tiles_per_seq == 0)
    def _():
        ubuf[0:8, :] = jnp.zeros((8, D_CONV), F32)

    ubuf[8:8 + tm, :] = u
    cw = cw_ref[...]
    conv = (cw[0:1, :] * ubuf[6:6 + tm, :] + cw[1:2, :] * ubuf[7:7 + tm, :]) + cw[2:3, :] * u
    ubuf[0:8, :] = ubuf[tm:tm + 8, :]
    ua = (p[:, D_CONV:2 * D_CONV] * conv).astype(BF16)

    o = 3 * D_CONV
    qkv = _dot(h, win_ref[:, o:o + 3 * D_ATTN])
    o += 3 * D_ATTN
    ga = _dot(h, win_ref[:, o:o + D_MODEL])
    gb = _dot(h, win_ref[:, o + D_MODEL:o + 2 * D_MODEL])

    first = lax.broadcasted_iota(jnp.int32, (tm, LANES), 1) < HEAD_DIM
    q_scale = (HEAD_DIM ** -0.5) * math.log2(math.e)
    for pr in range(N_PAIRS):
        qs = qkv[:, pr * LANES:(pr + 1) * LANES] * q_scale
        vs = qkv[:, 2 * D_ATTN + pr * LANES:2 * D_ATTN + (pr + 1) * LANES]
        lo, mid, hi = 2 * pr * LANES, (2 * pr + 1) * LANES, (2 * pr + 2) * LANES
        qz_ref[:, lo:mid] = jnp.where(first, qs, 0.0).astype(BF16)
        qz_ref[:, mid:hi] = jnp.where(first, 0.0, qs).astype(BF16)
        vz_ref[:, lo:mid] = jnp.where(first, vs, 0.0).astype(BF16)
        vz_ref[:, mid:hi] = jnp.where(first, 0.0, vs).astype(BF16)
    k_ref[...] = qkv[:, D_ATTN:2 * D_ATTN].astype(BF16)
    gbs_ref[...] = jax.nn.sigmoid(gb).astype(BF16)
    ma_ref[...] = (jax.nn.sigmoid(ga) * _dot(ua, wa_ref[...])).astype(BF16)


def _mix_in(x2, g, win, cw, wa, seq):
    t = x2.shape[0]
    tm = min(512, seq)
    n_in = win.shape[1]
    body = functools.partial(_mix_in_body, tm=tm, tiles_per_seq=seq // tm)
    row = lambda w: pl.BlockSpec((tm, w), lambda i: (i, 0))
    return pl.pallas_call(
        body,
        grid=(t // tm,),
        in_specs=[row(D_MODEL), _const_spec((1, D_MODEL)), _const_spec((D_MODEL, n_in)),
                  _const_spec((CONV_WIDTH, D_CONV)), _const_spec((D_CONV, D_MODEL))],
        out_specs=[row(D_SPLIT), row(D_ATTN), row(D_SPLIT), row(D_MODEL), row(D_MODEL)],
        out_shape=[jax.ShapeDtypeStruct((t, D_SPLIT), BF16),
                   jax.ShapeDtypeStruct((t, D_ATTN), BF16),
                   jax.ShapeDtypeStruct((t, D_SPLIT), BF16),
                   jax.ShapeDtypeStruct((t, D_MODEL), BF16),
                   jax.ShapeDtypeStruct((t, D_MODEL), BF16)],
        scratch_shapes=[pltpu.VMEM((tm + 8, D_CONV), F32)],
        compiler_params=_params(1),
        name="mix_in",
    )(x2, g, win, cw, wa)


def _attn_body(qz_ref, k_ref, vz_ref, o_ref, r_scr, o_scr, m_scr):
    i = pl.program_id(1)
    tb = ATTN_BLOCK
    row2 = lax.broadcasted_iota(jnp.int32, (2 * tb, tb), 0)
    col2 = lax.broadcasted_iota(jnp.int32, (2 * tb, tb), 1)
    causal2 = col2 < jnp.where(row2 >= tb, row2 - tb, row2)
    rj = lax.broadcasted_iota(jnp.int32, (tb, 2 * tb), 0)
    cs_ = lax.broadcasted_iota(jnp.int32, (tb, 2 * tb), 1)
    suffix = jnp.where(jnp.logical_or(rj > cs_, cs_ >= tb), 1.0, 0.0).astype(BF16)

    def sweep(steps, groups, fresh):
        chains = [(g, p) for g in groups for p in range(N_PAIRS)]
        start = {(g, s): pl.multiple_of((ATTN_GROUP * i + g - n) * tb, tb)
                 for g in groups for s, n in enumerate(steps)}
        diag = [isinstance(n, int) and n == 0 for n in steps]
        zs, log_betas, log_1ms, cs = {}, {}, {}, {}
        rs = {ci: (None if fresh else r_scr[g * N_PAIRS + p]) for ci, (g, p) in enumerate(chains)}
        os_ = {ci: (None if fresh else o_scr[g * N_PAIRS + p]) for ci, (g, p) in enumerate(chains)}

        def scores(s):
            for ci, (g, p) in enumerate(chains):
                rows = slice(g * tb, (g + 1) * tb)
                q2 = jnp.concatenate(
                    [qz_ref[0, rows, 2 * p * LANES:(2 * p + 1) * LANES],
                     qz_ref[0, rows, (2 * p + 1) * LANES:(2 * p + 2) * LANES]], axis=0)
                kp = k_ref[0, pl.ds(start[g, s], tb), p * LANES:(p + 1) * LANES]
                zs[s, ci] = lax.dot_general(q2, kp, (((1,), (1,)), ((), ())),
                                            preferred_element_type=F32)

        def logs_and_suffix_sums(s):
            parts = []
            for ci in range(len(chains)):
                z = zs[s, ci]
                sp = jnp.log2(1.0 + jnp.exp2(-jnp.abs(z)))
                log_beta = jnp.minimum(z, 0.0) - sp
                log_1m = log_beta - z
                if diag[s]:
                    log_1m = jnp.where(causal2, log_1m, 0.0)
                log_betas[s, ci] = log_beta
                parts.append(log_1m.astype(BF16))
            cs[s] = _dot(jnp.concatenate(parts, axis=0), suffix)

        def weights_and_values(s):
            for ci, (g, p) in enumerate(chains):
                c = cs[s][ci * 2 * tb:(ci + 1) * 2 * tb]
                r = rs[ci]
                t = log_betas[s, ci] + c[:, 0:tb]
                a = jnp.exp2(t if r is None else t + r)
                if diag[s]:
                    a = jnp.where(causal2, a, 0.0)
                rs[ci] = c[:, tb:2 * tb] if r is None else r + c[:, tb:2 * tb]
                a2 = jnp.concatenate([a[0:tb], a[tb:2 * tb]], axis=1).astype(BF16)
                v2 = jnp.concatenate(
                    [vz_ref[0, pl.ds(start[g, s], tb), 2 * p * LANES:(2 * p + 1) * LANES],
                     vz_ref[0, pl.ds(start[g, s], tb), (2 * p + 1) * LANES:(2 * p + 2) * LANES]],
                    axis=0)
                pv = _dot(a2, v2)
                os_[ci] = pv if os_[ci] is None else os_[ci] + pv

        scores(0)
        for s in range(len(steps)):
            if s + 1 < len(steps):
                scores(s + 1)
            logs_and_suffix_sums(s)
            if s >= 1:
                weights_and_values(s - 1)
        weights_and_values(len(steps) - 1)

        rmax = None
        for ci, (g, p) in enumerate(chains):
            r_scr[g * N_PAIRS + p] = rs[ci]
            o_scr[g * N_PAIRS + p] = os_[ci]
            rmax = rs[ci] if rmax is None else jnp.maximum(rmax, rs[ci])
        return jnp.max(rmax, axis=0, keepdims=True)[0, 0]

    all_groups = list(range(ATTN_GROUP))
    fused = list(range(min(ATTN_FUSED_STEPS, ATTN_GROUP + 1)))

    def tails(first_extra):
        for extra in range(first_extra, ATTN_GROUP):
            sweep([ATTN_GROUP * i + extra], all_groups[extra:], False)

    @pl.when(i == 0)
    def _():
        m_scr[0] = sweep([0], all_groups, True)

        @pl.when(m_scr[0] > LOG2_F32_ZERO)
        def _():
            tails(1)

    @pl.when(i > 0)
    def _():
        m_scr[0] = sweep(fused, all_groups, True)

    def cond(c):
        n, m = c
        return jnp.logical_and(n <= ATTN_GROUP * i, m > LOG2_F32_ZERO)

    def body(c):
        n, _ = c
        return n + 1, sweep([n], all_groups, False)

    n_end, m_end = lax.while_loop(cond, body, (jnp.int32(len(fused)), m_scr[0]))

    @pl.when(jnp.logical_and(i > 0, jnp.logical_and(n_end == ATTN_GROUP * i + 1,
                                                     m_end > LOG2_F32_ZERO)))
    def _():
        tails(1)

    for g in range(ATTN_GROUP):
        for p in range(N_PAIRS):
            o_ref[0, g * tb:(g + 1) * tb, p * LANES:(p + 1) * LANES] = (
                o_scr[g * N_PAIRS + p].astype(o_ref.dtype))


def _attention(qz, k, vz):
    b, s, _ = k.shape
    tb = ATTN_BLOCK
    tq = ATTN_GROUP * tb
    n_chains = ATTN_GROUP * N_PAIRS
    return pl.pallas_call(
        _attn_body,
        grid=(b, s // tq),
        in_specs=[pl.BlockSpec((1, tq, D_SPLIT), lambda bi, i: (bi, i, 0)),
                  pl.BlockSpec((1, s, D_ATTN), lambda bi, i: (bi, 0, 0)),
                  pl.BlockSpec((1, s, D_SPLIT), lambda bi, i: (bi, 0, 0))],
        out_specs=pl.BlockSpec((1, tq, D_ATTN), lambda bi, i: (bi, i, 0)),
        out_shape=jax.ShapeDtypeStruct((b, s, D_ATTN), BF16),
        scratch_shapes=[pltpu.VMEM((n_chains, 2 * tb, tb), F32),
                        pltpu.VMEM((n_chains, tb, LANES), F32),
                        pltpu.SMEM((1,), F32)],
        compiler_params=_params(2),
        name="stickbreak_attn",
    )(qz, k, vz)


def _merge(x_ref, ma_ref, gbs_ref, ub_ref, wb_ref, wo_ref, g_ref):
    mix = ma_ref[...].astype(F32) + gbs_ref[...].astype(F32) * _dot(ub_ref[...], wb_ref[...])
    xn = x_ref[...] + _dot(mix.astype(BF16), wo_ref[...])
    return xn, _rms(xn, g_ref[...])


def _mix_out_dense_body(x_ref, ma_ref, gbs_ref, ub_ref, wb_ref, wo_ref, g_ref,
                        wg_ref, wu_ref, wd_ref, o_ref):
    xn, h = _merge(x_ref, ma_ref, gbs_ref, ub_ref, wb_ref, wo_ref, g_ref)
    h = h.astype(BF16)
    act = (jax.nn.silu(_dot(h, wg_ref[...])) * _dot(h, wu_ref[...])).astype(BF16)
    o_ref[...] = xn + _dot(act, wd_ref[...])


def _mix_out_router_body(x_ref, ma_ref, gbs_ref, ub_ref, wb_ref, wo_ref, g_ref, wr_ref,
                         xo_ref, h_ref, lg_ref, *, tm):
    xn, h = _merge(x_ref, ma_ref, gbs_ref, ub_ref, wb_ref, wo_ref, g_ref)
    xo_ref[...] = xn
    for c in range(ROW_CHUNKS):
        h_ref[pl.ds(c, tm, stride=ROW_CHUNKS), :] = h[:, c * LANES:(c + 1) * LANES]
    h_hi = h.astype(BF16)
    h_lo = (h - h_hi.astype(F32)).astype(BF16)
    c2 = _dot(h_hi, wr_ref[...])
    lg_ref[...] = (c2[:, 0:ROUTER_PAD] + c2[:, ROUTER_PAD:2 * ROUTER_PAD]
                   + _dot(h_lo, wr_ref[:, 0:ROUTER_PAD]))


def _mix_out_specs(tm):
    row = lambda w: pl.BlockSpec((tm, w), lambda i: (i, 0))
    return [row(D_MODEL), row(D_MODEL), row(D_MODEL), row(D_ATTN),
            _const_spec((D_ATTN, D_MODEL)), _const_spec((D_MODEL, D_MODEL)),
            _const_spec((1, D_MODEL))]


def _mix_out_dense(x2, ma, gbs, ub, wb, wo, g, wg, wu, wd):
    t = x2.shape[0]
    tm = min(512, t)
    dff = wg.shape[1]
    return pl.pallas_call(
        _mix_out_dense_body,
        grid=(t // tm,),
        in_specs=_mix_out_specs(tm) + [_const_spec((D_MODEL, dff)), _const_spec((D_MODEL, dff)),
                                       _const_spec((dff, D_MODEL))],
        out_specs=pl.BlockSpec((tm, D_MODEL), lambda i: (i, 0)),
        out_shape=jax.ShapeDtypeStruct((t, D_MODEL), F32),
        compiler_params=_params(1),
        name="mix_out_dense_ffn",
    )(x2, ma, gbs, ub, wb, wo, g, wg, wu, wd)


def _mix_out_router(x2, ma, gbs, ub, wb, wo, g, wr):
    t = x2.shape[0]
    tm = min(512, t)
    row = lambda w: pl.BlockSpec((tm, w), lambda i: (i, 0))
    return pl.pallas_call(
        functools.partial(_mix_out_router_body, tm=tm),
        grid=(t // tm,),
        in_specs=_mix_out_specs(tm) + [_const_spec((D_MODEL, 2 * ROUTER_PAD))],
        out_specs=[row(D_MODEL), pl.BlockSpec((tm * ROW_CHUNKS, LANES), lambda i: (i, 0)),
                   row(ROUTER_PAD)],
        out_shape=[jax.ShapeDtypeStruct((t, D_MODEL), F32),
                   jax.ShapeDtypeStruct((t * ROW_CHUNKS, LANES), F32),
                   jax.ShapeDtypeStruct((t, ROUTER_PAD), F32)],
        compiler_params=_params(1),
        name="mix_out_router",
    )(x2, ma, gbs, ub, wb, wo, g, wr)


def _route(logits, tm):
    t = logits.shape[0]
    top_vals, top_idx = lax.top_k(logits, TOP_K)
    e_flat = top_idx.T.reshape(-1).astype(jnp.int32)
    experts = jnp.arange(N_EXPERTS, dtype=jnp.int32)
    onehot = (e_flat[:, None] == experts[None, :]).astype(jnp.int32)
    csum = jnp.cumsum(onehot, axis=0)
    counts = csum[-1]
    starts = jnp.cumsum(counts) - counts
    pos = jnp.sum((csum - 1 + starts[None, :]) * onehot, axis=1).astype(jnp.int32)
    order = _invert_permutation(pos)
    tiles_e = (counts + tm - 1) // tm
    tile_end = jnp.cumsum(tiles_e)
    tile_begin = tile_end - tiles_e
    n_tiles = (TOP_K * t) // tm + N_EXPERTS
    ti = jnp.arange(n_tiles, dtype=jnp.int32)
    tile_expert = jnp.minimum(
        jnp.sum((ti[:, None] >= tile_end[None, :]).astype(jnp.int32), axis=1), N_EXPERTS - 1)
    local = ti - tile_begin[tile_expert]
    row0 = starts[tile_expert] + local * tm
    n_rows = jnp.clip(counts[tile_expert] - local * tm, 0, tm)
    r = jnp.arange(tm, dtype=jnp.int32)[None, :]
    a = order[jnp.minimum(row0[:, None] + r, TOP_K * t - 1)]
    row_token = jnp.where(r < n_rows[:, None], jnp.where(a >= t, a - t, a), 0).reshape(-1)
    row_of = jnp.sum((csum - 1 + (tile_begin * tm)[None, :]) * onehot, axis=1).astype(jnp.int32)
    n_valid = tile_end[-1].astype(jnp.int32).reshape(1)
    return top_vals, row_token.astype(jnp.int32), row_of, tile_expert.astype(jnp.int32), n_valid, n_tiles


def _invert_body(pos_ref, order_ref, *, n):
    def body(a, c):
        order_ref[pos_ref[a]] = a
        return c
    lax.fori_loop(0, n, body, 0, unroll=16)


def _invert_permutation(pos):
    n = pos.shape[0]
    return pl.pallas_call(
        functools.partial(_invert_body, n=n),
        in_specs=[pl.BlockSpec(memory_space=pltpu.SMEM)],
        out_specs=pl.BlockSpec(memory_space=pltpu.SMEM),
        out_shape=jax.ShapeDtypeStruct((n,), jnp.int32),
        name="moe_invert",
    )(pos)


def _row_tile(r):
    return pl.ds(pl.multiple_of(r * ROW_CHUNKS, ROW_CHUNKS), ROW_CHUNKS)


ROW_DMA_UNROLL = 32


def _row_dma_loop(n_rows, issue):
    def body(g, c):
        for j in range(ROW_DMA_UNROLL):
            issue(g * ROW_DMA_UNROLL + j, j % 2)
        return c
    lax.fori_loop(0, n_rows // ROW_DMA_UNROLL, body, 0)


def _expert_body(te_ref, nv_ref, tok_ref, h_ref, wg_ref, wu_ref, wd_ref, y_ref, xbuf, gsem,
                 *, tm):
    del te_ref
    i = pl.program_id(0)
    n_valid = nv_ref[0]
    slot = i % 2
    rc = ROW_CHUNKS

    def gather_start(tile, sl):
        def issue(r, queue):
            pltpu.make_async_copy(h_ref.at[_row_tile(tok_ref[tile * tm + r])],
                                  xbuf.at[sl, _row_tile(r)], gsem.at[sl]).start(priority=queue)
        _row_dma_loop(tm, issue)

    @pl.when(jnp.logical_and(i == 0, n_valid > 0))
    def _():
        gather_start(0, 0)

    @pl.when(i + 1 < n_valid)
    def _():
        gather_start(i + 1, 1 - slot)

    @pl.when(i < n_valid)
    def _():
        pltpu.make_async_copy(h_ref.at[pl.ds(0, tm * rc)], xbuf.at[slot], gsem.at[slot]).wait()
        x = jnp.concatenate(
            [xbuf[slot, pl.ds(c, tm, stride=rc), :].astype(BF16) for c in range(rc)], axis=1)
        act = (jax.nn.silu(_dot(x, wg_ref[0])) * _dot(x, wu_ref[0])).astype(BF16)
        y = _dot(act, wd_ref[0])
        for c in range(rc):
            y_ref[pl.ds(c, tm, stride=rc), :] = y[:, c * LANES:(c + 1) * LANES]

    @pl.when(i >= n_valid)
    def _():
        y_ref[...] = jnp.zeros_like(y_ref)


def _experts(tile_expert, n_valid, row_token, h, wg, wu, wd, tm, n_tiles):
    dfe = wg.shape[2]
    wmap = lambda i, te, nv, tk: (te[i], 0, 0)
    return pl.pallas_call(
        functools.partial(_expert_body, tm=tm),
        grid_spec=pltpu.PrefetchScalarGridSpec(
            num_scalar_prefetch=3,
            grid=(n_tiles,),
            in_specs=[pl.BlockSpec(memory_space=pl.ANY),
                      pl.BlockSpec((1, D_MODEL, dfe), wmap),
                      pl.BlockSpec((1, D_MODEL, dfe), wmap),
                      pl.BlockSpec((1, dfe, D_MODEL), wmap)],
            out_specs=pl.BlockSpec((tm * ROW_CHUNKS, LANES), lambda i, te, nv, tk: (i, 0)),
            scratch_shapes=[pltpu.VMEM((2, tm * ROW_CHUNKS, LANES), F32),
                            pltpu.SemaphoreType.DMA((2,))],
        ),
        out_shape=jax.ShapeDtypeStruct((n_tiles * tm * ROW_CHUNKS, LANES), F32),
        compiler_params=_params(1),
        name="moe_experts",
    )(tile_expert, n_valid, row_token, h, wg, wu, wd)


def _combine_body(row_ref, x_ref, v_ref, g_ref, y_ref, o_ref, ybuf, sem,
                  *, tm, t, n_steps, final_norm):
    i = pl.program_id(0)
    slot = i % 2
    rc = ROW_CHUNKS

    def gather_start(step, sl):
        def issue(r, queue):
            for k in range(TOP_K):
                pltpu.make_async_copy(y_ref.at[_row_tile(row_ref[k * t + step * tm + r])],
                                      ybuf.at[sl, _row_tile(k * tm + r)],
                                      sem.at[sl]).start(priority=(queue + k) % 2)
        _row_dma_loop(tm, issue)

    @pl.when(i == 0)
    def _():
        gather_start(0, 0)

    @pl.when(i + 1 < n_steps)
    def _():
        gather_start(i + 1, 1 - slot)

    pltpu.make_async_copy(y_ref.at[pl.ds(0, TOP_K * tm * rc)], ybuf.at[slot], sem.at[slot]).wait()
    rows = lambda k: jnp.concatenate(
        [ybuf[slot, pl.ds(k * tm * rc + c, tm, stride=rc), :] for c in range(rc)], axis=1)
    v = v_ref[...]
    e = jnp.exp(v - jnp.max(v, axis=-1, keepdims=True))
    w = e / jnp.sum(e, axis=-1, keepdims=True)
    xn = x_ref[...] + (w[:, 0:1] * rows(0) + w[:, 1:2] * rows(1))
    o_ref[...] = _rms(xn, g_ref[...]) if final_norm else xn


def _combine(row_of, x2, y, top_vals, g, final_norm):
    t = x2.shape[0]
    tm = min(512, t)
    n_steps = t // tm
    return pl.pallas_call(
        functools.partial(_combine_body, tm=tm, t=t, n_steps=n_steps, final_norm=final_norm),
        grid_spec=pltpu.PrefetchScalarGridSpec(
            num_scalar_prefetch=1,
            grid=(n_steps,),
            in_specs=[pl.BlockSpec((tm, D_MODEL), lambda i, ro: (i, 0)),
                      pl.BlockSpec((tm, TOP_K), lambda i, ro: (i, 0)),
                      pl.BlockSpec((1, D_MODEL), lambda i, ro: (0, 0)),
                      pl.BlockSpec(memory_space=pl.ANY)],
            out_specs=pl.BlockSpec((tm, D_MODEL), lambda i, ro: (i, 0)),
            scratch_shapes=[pltpu.VMEM((2, TOP_K * tm * ROW_CHUNKS, LANES), F32),
                            pltpu.SemaphoreType.DMA((2,))],
        ),
        out_shape=jax.ShapeDtypeStruct((t, D_MODEL), F32),
        compiler_params=_params(1),
        name="moe_combine",
    )(row_of, x2, top_vals, g, y)


def _final_norm_body(x_ref, g_ref, o_ref):
    o_ref[...] = _rms(x_ref[...], g_ref[...])


def _final_norm(x2, g):
    t = x2.shape[0]
    tm = min(512, t)
    return pl.pallas_call(
        _final_norm_body,
        grid=(t // tm,),
        in_specs=[pl.BlockSpec((tm, D_MODEL), lambda i: (i, 0)),
                  pl.BlockSpec((1, D_MODEL), lambda i: (0, 0))],
        out_specs=pl.BlockSpec((tm, D_MODEL), lambda i: (i, 0)),
        out_shape=jax.ShapeDtypeStruct((t, D_MODEL), F32),
        compiler_params=_params(1),
        name="final_norm",
    )(x2, g)


def _router_weights(w):
    wp = jnp.pad(w, ((0, 0), (0, ROUTER_PAD - N_EXPERTS)))
    hi = wp.astype(BF16)
    lo = (wp - hi.astype(F32)).astype(BF16)
    return jnp.concatenate([hi, lo], axis=1)


def kernel(x, g_mix, w_in, conv_w, w_branch_conv, w_branch_attn, w_out, g_ffn, w_ffn_gate, w_ffn_up, w_ffn_down, w_router, w_exp_gate, w_exp_up, w_exp_down, g_final):
    b, s, d = x.shape
    depth = g_mix.shape[0]
    t = b * s
    x2 = x.reshape(t, d)
    gf = g_final.reshape(1, d)
    for i in range(depth):
        last = i == depth - 1
        moe = i % 2 == 1
        j = i // 2
        qz, k, vz, ma, gbs = _mix_in(x2, g_mix[i].reshape(1, d), w_in[i].astype(BF16), conv_w[i],
                                     w_branch_conv[i].astype(BF16), s)
        ub = _attention(qz.reshape(b, s, D_SPLIT), k.reshape(b, s, D_ATTN),
                        vz.reshape(b, s, D_SPLIT)).reshape(t, D_ATTN)
        wb = w_branch_attn[i].astype(BF16)
        wo = w_out[i].astype(BF16)
        gi = g_ffn[i].reshape(1, d)
        if not moe:
            x2 = _mix_out_dense(x2, ma, gbs, ub, wb, wo, gi, w_ffn_gate[j].astype(BF16),
                                w_ffn_up[j].astype(BF16), w_ffn_down[j].astype(BF16))
            if last:
                x2 = _final_norm(x2, gf)
        else:
            x2, h, logits = _mix_out_router(x2, ma, gbs, ub, wb, wo, gi,
                                            _router_weights(w_router[j]))
            tm = min(512, t)
            top_vals, row_token, row_of, tile_expert, n_valid, n_tiles = _route(
                logits[:, :N_EXPERTS], tm)
            y = _experts(tile_expert, n_valid, row_token, h,
                         w_exp_gate[j].astype(BF16), w_exp_up[j].astype(BF16),
                         w_exp_down[j].astype(BF16), tm, n_tiles)
            x2 = _combine(row_of, x2, y, top_vals, gf, final_norm=last)
    return x2.reshape(b, s, d)
```

```python
import functools

import jax
import jax.numpy as jnp
from jax import lax
from jax.experimental import pallas as pl
from jax.experimental.pallas import tpu as pltpu

F32 = jnp.float32
BF16 = jnp.bfloat16

D_MODEL = 1024
D_CONV = 512
D_ATTN = 512
N_HEADS = 8
HEAD_DIM = 64
N_EXPERTS = 8
TOP_K = 2
EPS = 1e-6
CONV_WIDTH = 3

LANES = 128
VMEM_LIMIT_BYTES = 56 * 1024 * 1024

ATTN_BLOCK = 128
ATTN_GROUP = 4
ATTN_FUSED_STEPS = 3
N_PAIRS = N_HEADS // 2
D_SPLIT = 2 * D_ATTN
LOG_F32_ZERO = -104.0
ROUTER_PAD = 128
ROW_CHUNKS = D_MODEL // LANES


def _params(n_grid, vmem=VMEM_LIMIT_BYTES):
    return pltpu.CompilerParams(
        dimension_semantics=("arbitrary",) * n_grid, vmem_limit_bytes=vmem)


def _rms(x, g):
    return x * lax.rsqrt(jnp.mean(x * x, axis=-1, keepdims=True) + EPS) * g


def _dot(a, b):
    return jnp.dot(a, b, preferred_element_type=F32)


def _const_spec(shape):
    return pl.BlockSpec(shape, lambda i: (0,) * len(shape), pipeline_mode=pl.Buffered(1))


def _mix_in_body(x_ref, g_ref, win_ref, cw_ref, wa_ref,
                 qz_ref, k_ref, vz_ref, ma_ref, gbs_ref, ubuf, *, tm, tiles_per_seq):
    i = pl.program_id(0)
    h = _rms(x_ref[...], g_ref[...]).astype(BF16)

    p = _dot(h, win_ref[:, 0:3 * D_CONV])
    u = p[:, 2 * D_CONV:3 * D_CONV] * p[:, 0:D_CONV]

    @pl.when(i % tiles_per_seq == 0)
    def _():
        ubuf[0:8, :] = jnp.zeros((8, D_CONV), F32)

    ubuf[8:8 + tm, :] = u
    cw = cw_ref[...]
    conv = (cw[0:1, :] * ubuf[6:6 + tm, :] + cw[1:2, :] * ubuf[7:7 + tm, :]) + cw[2:3, :] * u
    ubuf[0:8, :] = ubuf[tm:tm + 8, :]
    ua = (p[:, D_CONV:2 * D_CONV] * conv).astype(BF16)

    o = 3 * D_CONV
    qkv = _dot(h, win_ref[:, o:o + 3 * D_ATTN])
    o += 3 * D_ATTN
    ga = _dot(h, win_ref[:, o:o + D_MODEL])
    gb = _dot(h, win_ref[:, o + D_MODEL:o + 2 * D_MODEL])

    first = lax.broadcasted_iota(jnp.int32, (tm, LANES), 1) < HEAD_DIM
    q_scale = HEAD_DIM ** -0.5
    for pr in range(N_PAIRS):
        qs = qkv[:, pr * LANES:(pr + 1) * LANES] * q_scale
        vs = qkv[:, 2 * D_ATTN + pr * LANES:2 * D_ATTN + (pr + 1) * LANES]
        lo, mid, hi = 2 * pr * LANES, (2 * pr + 1) * LANES, (2 * pr + 2) * LANES
        qz_ref[:, lo:mid] = jnp.where(first, qs, 0.0).astype(BF16)
        qz_ref[:, mid:hi] = jnp.where(first, 0.0, qs).astype(BF16)
        vz_ref[:, lo:mid] = jnp.where(first, vs, 0.0).astype(BF16)
        vz_ref[:, mid:hi] = jnp.where(first, 0.0, vs).astype(BF16)
    k_ref[...] = qkv[:, D_ATTN:2 * D_ATTN].astype(BF16)
    gbs_ref[...] = jax.nn.sigmoid(gb).astype(BF16)
    ma_ref[...] = (jax.nn.sigmoid(ga) * _dot(ua, wa_ref[...])).astype(BF16)


def _mix_in(x2, g, win, cw, wa, seq):
    t = x2.shape[0]
    tm = min(512, seq)
    n_in = win.shape[1]
    body = functools.partial(_mix_in_body, tm=tm, tiles_per_seq=seq // tm)
    row = lambda w: pl.BlockSpec((tm, w), lambda i: (i, 0))
    return pl.pallas_call(
        body,
        grid=(t // tm,),
        in_specs=[row(D_MODEL), _const_spec((1, D_MODEL)), _const_spec((D_MODEL, n_in)),
                  _const_spec((CONV_WIDTH, D_CONV)), _const_spec((D_CONV, D_MODEL))],
        out_specs=[row(D_SPLIT), row(D_ATTN), row(D_SPLIT), row(D_MODEL), row(D_MODEL)],
        out_shape=[jax.ShapeDtypeStruct((t, D_SPLIT), BF16),
                   jax.ShapeDtypeStruct((t, D_ATTN), BF16),
                   jax.ShapeDtypeStruct((t, D_SPLIT), BF16),
                   jax.ShapeDtypeStruct((t, D_MODEL), BF16),
                   jax.ShapeDtypeStruct((t, D_MODEL), BF16)],
        scratch_shapes=[pltpu.VMEM((tm + 8, D_CONV), F32)],
        compiler_params=_params(1),
        name="mix_in",
    )(x2, g, win, cw, wa)


def _attn_body(qz_ref, k_ref, vz_ref, o_ref, r_scr, o_scr, m_scr):
    i = pl.program_id(1)
    tb = ATTN_BLOCK
    row2 = lax.broadcasted_iota(jnp.int32, (2 * tb, tb), 0)
    col2 = lax.broadcasted_iota(jnp.int32, (2 * tb, tb), 1)
    causal2 = col2 < jnp.where(row2 >= tb, row2 - tb, row2)
    rj = lax.broadcasted_iota(jnp.int32, (tb, 2 * tb), 0)
    cs_ = lax.broadcasted_iota(jnp.int32, (tb, 2 * tb), 1)
    suffix = jnp.where(jnp.logical_or(rj > cs_, cs_ >= tb), 1.0, 0.0).astype(BF16)

    def sweep(steps, groups, fresh):
        chains = [(g, p) for g in groups for p in range(N_PAIRS)]
        start = {(g, s): pl.multiple_of((ATTN_GROUP * i + g - n) * tb, tb)
                 for g in groups for s, n in enumerate(steps)}
        diag = [isinstance(n, int) and n == 0 for n in steps]
        zs, log_betas, log_1ms, cs = {}, {}, {}, {}
        rs = {ci: (None if fresh else r_scr[g * N_PAIRS + p]) for ci, (g, p) in enumerate(chains)}
        os_ = {ci: (None if fresh else o_scr[g * N_PAIRS + p]) for ci, (g, p) in enumerate(chains)}

        def scores(s):
            for ci, (g, p) in enumerate(chains):
                rows = slice(g * tb, (g + 1) * tb)
                q2 = jnp.concatenate(
                    [qz_ref[0, rows, 2 * p * LANES:(2 * p + 1) * LANES],
                     qz_ref[0, rows, (2 * p + 1) * LANES:(2 * p + 2) * LANES]], axis=0)
                kp = k_ref[0, pl.ds(start[g, s], tb), p * LANES:(p + 1) * LANES]
                zs[s, ci] = lax.dot_general(q2, kp, (((1,), (1,)), ((), ())),
                                            preferred_element_type=F32)

        def logs_and_suffix_sums(s):
            parts = []
            for ci in range(len(chains)):
                z = zs[s, ci]
                sp = jnp.log(1.0 + jnp.exp(-jnp.abs(z)))
                log_beta = jnp.minimum(z, 0.0) - sp
                log_1m = log_beta - z
                if diag[s]:
                    log_1m = jnp.where(causal2, log_1m, 0.0)
                log_betas[s, ci] = log_beta
                parts.append(log_1m.astype(BF16))
            cs[s] = _dot(jnp.concatenate(parts, axis=0), suffix)

        def weights_and_values(s):
            for ci, (g, p) in enumerate(chains):
                c = cs[s][ci * 2 * tb:(ci + 1) * 2 * tb]
                r = rs[ci]
                t = log_betas[s, ci] + c[:, 0:tb]
                a = jnp.exp(t if r is None else t + r)
                if diag[s]:
                    a = jnp.where(causal2, a, 0.0)
                rs[ci] = c[:, tb:2 * tb] if r is None else r + c[:, tb:2 * tb]
                a2 = jnp.concatenate([a[0:tb], a[tb:2 * tb]], axis=1).astype(BF16)
                v2 = jnp.concatenate(
                    [vz_ref[0, pl.ds(start[g, s], tb), 2 * p * LANES:(2 * p + 1) * LANES],
                     vz_ref[0, pl.ds(start[g, s], tb), (2 * p + 1) * LANES:(2 * p + 2) * LANES]],
                    axis=0)
                pv = _dot(a2, v2)
                os_[ci] = pv if os_[ci] is None else os_[ci] + pv

        scores(0)
        for s in range(len(steps)):
            if s + 1 < len(steps):
                scores(s + 1)
            logs_and_suffix_sums(s)
            if s >= 1:
                weights_and_values(s - 1)
        weights_and_values(len(steps) - 1)

        rmax = None
        for ci, (g, p) in enumerate(chains):
            r_scr[g * N_PAIRS + p] = rs[ci]
            o_scr[g * N_PAIRS + p] = os_[ci]
            rmax = rs[ci] if rmax is None else jnp.maximum(rmax, rs[ci])
        return jnp.max(rmax, axis=0, keepdims=True)[0, 0]

    all_groups = list(range(ATTN_GROUP))
    fused = list(range(min(ATTN_FUSED_STEPS, ATTN_GROUP + 1)))

    def tails(first_extra):
        for extra in range(first_extra, ATTN_GROUP):
            sweep([ATTN_GROUP * i + extra], all_groups[extra:], False)

    @pl.when(i == 0)
    def _():
        m_scr[0] = sweep([0], all_groups, True)

        @pl.when(m_scr[0] > LOG_F32_ZERO)
        def _():
            tails(1)

    @pl.when(i > 0)
    def _():
        m_scr[0] = sweep(fused, all_groups, True)

    def cond(c):
        n, m = c
        return jnp.logical_and(n <= ATTN_GROUP * i, m > LOG_F32_ZERO)

    def body(c):
        n, _ = c
        return n + 1, sweep([n], all_groups, False)

    n_end, m_end = lax.while_loop(cond, body, (jnp.int32(len(fused)), m_scr[0]))

    @pl.when(jnp.logical_and(i > 0, jnp.logical_and(n_end == ATTN_GROUP * i + 1,
                                                     m_end > LOG_F32_ZERO)))
    def _():
        tails(1)

    for g in range(ATTN_GROUP):
        for p in range(N_PAIRS):
            o_ref[0, g * tb:(g + 1) * tb, p * LANES:(p + 1) * LANES] = (
                o_scr[g * N_PAIRS + p].astype(o_ref.dtype))


def _attention(qz, k, vz):
    b, s, _ = k.shape
    tb = ATTN_BLOCK
    tq = ATTN_GROUP * tb
    n_chains = ATTN_GROUP * N_PAIRS
    return pl.pallas_call(
        _attn_body,
        grid=(b, s // tq),
        in_specs=[pl.BlockSpec((1, tq, D_SPLIT), lambda bi, i: (bi, i, 0)),
                  pl.BlockSpec((1, s, D_ATTN), lambda bi, i: (bi, 0, 0)),
                  pl.BlockSpec((1, s, D_SPLIT), lambda bi, i: (bi, 0, 0))],
        out_specs=pl.BlockSpec((1, tq, D_ATTN), lambda bi, i: (bi, i, 0)),
        out_shape=jax.ShapeDtypeStruct((b, s, D_ATTN), BF16),
        scratch_shapes=[pltpu.VMEM((n_chains, 2 * tb, tb), F32),
                        pltpu.VMEM((n_chains, tb, LANES), F32),
                        pltpu.SMEM((1,), F32)],
        compiler_params=_params(2),
        name="stickbreak_attn",
    )(qz, k, vz)


def _merge(x_ref, ma_ref, gbs_ref, ub_ref, wb_ref, wo_ref, g_ref):
    mix = ma_ref[...].astype(F32) + gbs_ref[...].astype(F32) * _dot(ub_ref[...], wb_ref[...])
    xn = x_ref[...] + _dot(mix.astype(BF16), wo_ref[...])
    return xn, _rms(xn, g_ref[...])


def _mix_out_dense_body(x_ref, ma_ref, gbs_ref, ub_ref, wb_ref, wo_ref, g_ref,
                        wg_ref, wu_ref, wd_ref, o_ref):
    xn, h = _merge(x_ref, ma_ref, gbs_ref, ub_ref, wb_ref, wo_ref, g_ref)
    h = h.astype(BF16)
    act = (jax.nn.silu(_dot(h, wg_ref[...])) * _dot(h, wu_ref[...])).astype(BF16)
    o_ref[...] = xn + _dot(act, wd_ref[...])


def _mix_out_router_body(x_ref, ma_ref, gbs_ref, ub_ref, wb_ref, wo_ref, g_ref, wr_ref,
                         xo_ref, h_ref, lg_ref, *, tm):
    xn, h = _merge(x_ref, ma_ref, gbs_ref, ub_ref, wb_ref, wo_ref, g_ref)
    xo_ref[...] = xn
    for c in range(ROW_CHUNKS):
        h_ref[pl.ds(c, tm, stride=ROW_CHUNKS), :] = h[:, c * LANES:(c + 1) * LANES]
    h_hi = h.astype(BF16)
    h_lo = (h - h_hi.astype(F32)).astype(BF16)
    c2 = _dot(h_hi, wr_ref[...])
    lg_ref[...] = (c2[:, 0:ROUTER_PAD] + c2[:, ROUTER_PAD:2 * ROUTER_PAD]
                   + _dot(h_lo, wr_ref[:, 0:ROUTER_PAD]))


def _mix_out_specs(tm):
    row = lambda w: pl.BlockSpec((tm, w), lambda i: (i, 0))
    return [row(D_MODEL), row(D_MODEL), row(D_MODEL), row(D_ATTN),
            _const_spec((D_ATTN, D_MODEL)), _const_spec((D_MODEL, D_MODEL)),
            _const_spec((1, D_MODEL))]


def _mix_out_dense(x2, ma, gbs, ub, wb, wo, g, wg, wu, wd):
    t = x2.shape[0]
    tm = min(512, t)
    dff = wg.shape[1]
    return pl.pallas_call(
        _mix_out_dense_body,
        grid=(t // tm,),
        in_specs=_mix_out_specs(tm) + [_const_spec((D_MODEL, dff)), _const_spec((D_MODEL, dff)),
                                       _const_spec((dff, D_MODEL))],
        out_specs=pl.BlockSpec((tm, D_MODEL), lambda i: (i, 0)),
        out_shape=jax.ShapeDtypeStruct((t, D_MODEL), F32),
        compiler_params=_params(1),
        name="mix_out_dense_ffn",
    )(x2, ma, gbs, ub, wb, wo, g, wg, wu, wd)


def _mix_out_router(x2, ma, gbs, ub, wb, wo, g, wr):
    t = x2.shape[0]
    tm = min(512, t)
    row = lambda w: pl.BlockSpec((tm, w), lambda i: (i, 0))
    return pl.pallas_call(
        functools.partial(_mix_out_router_body, tm=tm),
        grid=(t // tm,),
        in_specs=_mix_out_specs(tm) + [_const_spec((D_MODEL, 2 * ROUTER_PAD))],
        out_specs=[row(D_MODEL), pl.BlockSpec((tm * ROW_CHUNKS, LANES), lambda i: (i, 0)),
                   row(ROUTER_PAD)],
        out_shape=[jax.ShapeDtypeStruct((t, D_MODEL), F32),
                   jax.ShapeDtypeStruct((t * ROW_CHUNKS, LANES), F32),
                   jax.ShapeDtypeStruct((t, ROUTER_PAD), F32)],
        compiler_params=_params(1),
        name="mix_out_router",
    )(x2, ma, gbs, ub, wb, wo, g, wr)


def _route(logits, tm):
    t = logits.shape[0]
    top_vals, top_idx = lax.top_k(logits, TOP_K)
    e_flat = top_idx.T.reshape(-1).astype(jnp.int32)
    experts = jnp.arange(N_EXPERTS, dtype=jnp.int32)
    onehot = (e_flat[:, None] == experts[None, :]).astype(jnp.int32)
    csum = jnp.cumsum(onehot, axis=0)
    counts = csum[-1]
    tiles_e = (counts + tm - 1) // tm
    tile_end = jnp.cumsum(tiles_e)
    tile_begin = tile_end - tiles_e
    n_tiles = (TOP_K * t) // tm + N_EXPERTS
    ti = jnp.arange(n_tiles, dtype=jnp.int32)
    tile_expert = jnp.minimum(
        jnp.sum((ti[:, None] >= tile_end[None, :]).astype(jnp.int32), axis=1), N_EXPERTS - 1)
    row_of = jnp.sum((csum - 1 + (tile_begin * tm)[None, :]) * onehot, axis=1).astype(jnp.int32)
    n_valid = tile_end[-1]
    tail = n_valid + experts
    fill_tiles = jnp.concatenate([jnp.where(tiles_e > 0, tile_end - 1, -1),
                                  jnp.where(tail < n_tiles, tail, -1)]).astype(jnp.int32)
    return (top_vals, row_of, tile_expert.astype(jnp.int32), fill_tiles,
            n_valid.astype(jnp.int32).reshape(1), n_tiles)


def _row_tile(r):
    return pl.ds(pl.multiple_of(r * ROW_CHUNKS, ROW_CHUNKS), ROW_CHUNKS)


ROW_DMA_UNROLL = 32


def _row_dma_loop(n_rows, issue):
    def body(g, c):
        for j in range(ROW_DMA_UNROLL):
            issue(g * ROW_DMA_UNROLL + j, j % 2)
        return c
    lax.fori_loop(0, n_rows // ROW_DMA_UNROLL, body, 0)


DISPATCH_BUFFERS = 3


def _dispatch_body(row_ref, fill_ref, h_ref, xs_ref, hbuf, zbuf, lsem, ssem, zsem,
                   *, tm, t, n_steps):
    i = pl.program_id(0)
    nb = DISPATCH_BUFFERS
    slot = i % nb
    rc = ROW_CHUNKS
    tile = tm * rc

    def load(step, sl):
        return pltpu.make_async_copy(
            h_ref.at[pl.ds(pl.multiple_of(step * tile, tile), tile)], hbuf.at[sl], lsem.at[sl])

    def wait_rows(sl):
        for _ in range(TOP_K):
            pltpu.make_async_copy(hbuf.at[sl], xs_ref.at[pl.ds(0, tile)], ssem.at[sl]).wait()

    @pl.when(i == 0)
    def _():
        load(0, 0).start()
        zbuf[...] = jnp.zeros_like(zbuf)

        def fill(j):
            first_row = pl.multiple_of(jnp.maximum(fill_ref[j], 0) * tile, tile)
            return pltpu.make_async_copy(zbuf, xs_ref.at[pl.ds(first_row, tile)], zsem)

        for j in range(2 * N_EXPERTS):
            pl.when(fill_ref[j] >= 0)(fill(j).start)
        for j in range(2 * N_EXPERTS):
            pl.when(fill_ref[j] >= 0)(fill(j).wait)

    @pl.when(i >= 2)
    def _():
        wait_rows((i - 2) % nb)

    @pl.when(i + 1 < n_steps)
    def _():
        load(i + 1, (i + 1) % nb).start()

    load(i, slot).wait()

    def issue(r, queue):
        for k in range(TOP_K):
            pltpu.make_async_copy(hbuf.at[slot, _row_tile(r)],
                                  xs_ref.at[_row_tile(row_ref[k * t + i * tm + r])],
                                  ssem.at[slot]).start(priority=(queue + k) % 2)
    _row_dma_loop(tm, issue)

    @pl.when(i == n_steps - 1)
    def _():
        @pl.when(i >= 1)
        def _():
            wait_rows((i - 1) % nb)
        wait_rows(slot)


def _dispatch(row_of, fill_tiles, h, tm, n_tiles):
    t = h.shape[0] // ROW_CHUNKS
    n_steps = t // tm
    return pl.pallas_call(
        functools.partial(_dispatch_body, tm=tm, t=t, n_steps=n_steps),
        grid_spec=pltpu.PrefetchScalarGridSpec(
            num_scalar_prefetch=2,
            grid=(n_steps,),
            in_specs=[pl.BlockSpec(memory_space=pl.ANY)],
            out_specs=pl.BlockSpec(memory_space=pl.ANY),
            scratch_shapes=[pltpu.VMEM((DISPATCH_BUFFERS, tm * ROW_CHUNKS, LANES), F32),
                            pltpu.VMEM((tm * ROW_CHUNKS, LANES), F32),
                            pltpu.SemaphoreType.DMA((DISPATCH_BUFFERS,)),
                            pltpu.SemaphoreType.DMA((DISPATCH_BUFFERS,)),
                            pltpu.SemaphoreType.DMA],
        ),
        out_shape=jax.ShapeDtypeStruct((n_tiles * tm * ROW_CHUNKS, LANES), F32),
        compiler_params=_params(1),
        name="moe_dispatch",
    )(row_of, fill_tiles, h)


def _expert_body(te_ref, nv_ref, xs_ref, wg_ref, wu_ref, wd_ref, y_ref, *, tm):
    del te_ref
    i = pl.program_id(0)
    rc = ROW_CHUNKS

    @pl.when(i < nv_ref[0])
    def _():
        x = jnp.concatenate(
            [xs_ref[pl.ds(c, tm, stride=rc), :].astype(BF16) for c in range(rc)], axis=1)
        act = (jax.nn.silu(_dot(x, wg_ref[0])) * _dot(x, wu_ref[0])).astype(BF16)
        y = _dot(act, wd_ref[0])
        for c in range(rc):
            y_ref[pl.ds(c, tm, stride=rc), :] = y[:, c * LANES:(c + 1) * LANES]

    @pl.when(i >= nv_ref[0])
    def _():
        y_ref[...] = jnp.zeros_like(y_ref)


def _experts(tile_expert, n_valid, xs, wg, wu, wd, tm, n_tiles):
    dfe = wg.shape[2]
    wmap = lambda i, te, nv: (te[i], 0, 0)
    row = pl.BlockSpec((tm * ROW_CHUNKS, LANES), lambda i, te, nv: (i, 0))
    return pl.pallas_call(
        functools.partial(_expert_body, tm=tm),
        grid_spec=pltpu.PrefetchScalarGridSpec(
            num_scalar_prefetch=2,
            grid=(n_tiles,),
            in_specs=[row,
                      pl.BlockSpec((1, D_MODEL, dfe), wmap),
                      pl.BlockSpec((1, D_MODEL, dfe), wmap),
                      pl.BlockSpec((1, dfe, D_MODEL), wmap)],
            out_specs=row,
        ),
        out_shape=jax.ShapeDtypeStruct((n_tiles * tm * ROW_CHUNKS, LANES), F32),
        compiler_params=_params(1),
        name="moe_experts",
    )(tile_expert, n_valid, xs, wg, wu, wd)


def _combine_body(row_ref, x_ref, v_ref, g_ref, y_ref, o_ref, ybuf, sem,
                  *, tm, t, n_steps, final_norm):
    i = pl.program_id(0)
    slot = i % 2
    rc = ROW_CHUNKS

    def gather_start(step, sl):
        def issue(r, queue):
            for k in range(TOP_K):
                pltpu.make_async_copy(y_ref.at[_row_tile(row_ref[k * t + step * tm + r])],
                                      ybuf.at[sl, _row_tile(k * tm + r)],
                                      sem.at[sl]).start(priority=(queue + k) % 2)
        _row_dma_loop(tm, issue)

    @pl.when(i == 0)
    def _():
        gather_start(0, 0)

    @pl.when(i + 1 < n_steps)
    def _():
        gather_start(i + 1, 1 - slot)

    pltpu.make_async_copy(y_ref.at[pl.ds(0, TOP_K * tm * rc)], ybuf.at[slot], sem.at[slot]).wait()
    rows = lambda k: jnp.concatenate(
        [ybuf[slot, pl.ds(k * tm * rc + c, tm, stride=rc), :] for c in range(rc)], axis=1)
    v = v_ref[...]
    e = jnp.exp(v - jnp.max(v, axis=-1, keepdims=True))
    w = e / jnp.sum(e, axis=-1, keepdims=True)
    xn = x_ref[...] + (w[:, 0:1] * rows(0) + w[:, 1:2] * rows(1))
    o_ref[...] = _rms(xn, g_ref[...]) if final_norm else xn


def _combine(row_of, x2, y, top_vals, g, final_norm):
    t = x2.shape[0]
    tm = min(512, t)
    n_steps = t // tm
    return pl.pallas_call(
        functools.partial(_combine_body, tm=tm, t=t, n_steps=n_steps, final_norm=final_norm),
        grid_spec=pltpu.PrefetchScalarGridSpec(
            num_scalar_prefetch=1,
            grid=(n_steps,),
            in_specs=[pl.BlockSpec((tm, D_MODEL), lambda i, ro: (i, 0)),
                      pl.BlockSpec((tm, TOP_K), lambda i, ro: (i, 0)),
                      pl.BlockSpec((1, D_MODEL), lambda i, ro: (0, 0)),
                      pl.BlockSpec(memory_space=pl.ANY)],
            out_specs=pl.BlockSpec((tm, D_MODEL), lambda i, ro: (i, 0)),
            scratch_shapes=[pltpu.VMEM((2, TOP_K * tm * ROW_CHUNKS, LANES), F32),
                            pltpu.SemaphoreType.DMA((2,))],
        ),
        out_shape=jax.ShapeDtypeStruct((t, D_MODEL), F32),
        compiler_params=_params(1),
        name="moe_combine",
    )(row_of, x2, top_vals, g, y)


def _final_norm_body(x_ref, g_ref, o_ref):
    o_ref[...] = _rms(x_ref[...], g_ref[...])


def _final_norm(x2, g):
    t = x2.shape[0]
    tm = min(512, t)
    return pl.pallas_call(
        _final_norm_body,
        grid=(t // tm,),
        in_specs=[pl.BlockSpec((tm, D_MODEL), lambda i: (i, 0)),
                  pl.BlockSpec((1, D_MODEL), lambda i: (0, 0))],
        out_specs=pl.BlockSpec((tm, D_MODEL), lambda i: (i, 0)),
        out_shape=jax.ShapeDtypeStruct((t, D_MODEL), F32),
        compiler_params=_params(1),
        name="final_norm",
    )(x2, g)


def _router_weights(w):
    wp = jnp.pad(w, ((0, 0), (0, ROUTER_PAD - N_EXPERTS)))
    hi = wp.astype(BF16)
    lo = (wp - hi.astype(F32)).astype(BF16)
    return jnp.concatenate([hi, lo], axis=1)


def kernel(x, g_mix, w_in, conv_w, w_branch_conv, w_branch_attn, w_out, g_ffn, w_ffn_gate, w_ffn_up, w_ffn_down, w_router, w_exp_gate, w_exp_up, w_exp_down, g_final):
    b, s, d = x.shape
    depth = g_mix.shape[0]
    t = b * s
    x2 = x.reshape(t, d)
    gf = g_final.reshape(1, d)
    for i in range(depth):
        last = i == depth - 1
        moe = i % 2 == 1
        j = i // 2
        qz, k, vz, ma, gbs = _mix_in(x2, g_mix[i].reshape(1, d), w_in[i].astype(BF16), conv_w[i],
                                     w_branch_conv[i].astype(BF16), s)
        ub = _attention(qz.reshape(b, s, D_SPLIT), k.reshape(b, s, D_ATTN),
                        vz.reshape(b, s, D_SPLIT)).reshape(t, D_ATTN)
        wb = w_branch_attn[i].astype(BF16)
        wo = w_out[i].astype(BF16)
        gi = g_ffn[i].reshape(1, d)
        if not moe:
            x2 = _mix_out_dense(x2, ma, gbs, ub, wb, wo, gi, w_ffn_gate[j].astype(BF16),
                                w_ffn_up[j].astype(BF16), w_ffn_down[j].astype(BF16))
            if last:
                x2 = _final_norm(x2, gf)
        else:
            x2, h, logits = _mix_out_router(x2, ma, gbs, ub, wb, wo, gi,
                                            _router_weights(w_router[j]))
            tm = min(512, t)
            top_vals, row_of, tile_expert, fill_tiles, n_valid, n_tiles = _route(
                logits[:, :N_EXPERTS], tm)
            xs = _dispatch(row_of, fill_tiles, h, tm, n_tiles)
            y = _experts(tile_expert, n_valid, xs,
                         w_exp_gate[j].astype(BF16), w_exp_up[j].astype(BF16),
                         w_exp_down[j].astype(BF16), tm, n_tiles)
            x2 = _combine(row_of, x2, y, top_vals, gf, final_norm=last)
    return x2.reshape(b, s, d)
```

```python
import functools

import jax
import jax.numpy as jnp
from jax import lax
from jax.experimental import pallas as pl
from jax.experimental.pallas import tpu as pltpu

F32 = jnp.float32
BF16 = jnp.bfloat16

D_MODEL = 1024
D_CONV = 512
D_ATTN = 512
N_HEADS = 8
HEAD_DIM = 64
N_EXPERTS = 8
TOP_K = 2
EPS = 1e-6
CONV_WIDTH = 3

LANES = 128
VMEM_LIMIT_BYTES = 56 * 1024 * 1024

ATTN_BLOCK = 128
ATTN_GROUP = 4
ATTN_FUSED_STEPS = 3
N_PAIRS = N_HEADS // 2
D_SPLIT = 2 * D_ATTN
LOG_F32_ZERO = -104.0
ROUTER_PAD = 128
ROW_CHUNKS = D_MODEL // LANES


def _params(n_grid, vmem=VMEM_LIMIT_BYTES):
    return pltpu.CompilerParams(
        dimension_semantics=("arbitrary",) * n_grid, vmem_limit_bytes=vmem)


def _rms(x, g):
    return x * lax.rsqrt(jnp.mean(x * x, axis=-1, keepdims=True) + EPS) * g


def _dot(a, b):
    return jnp.dot(a, b, preferred_element_type=F32)


def _const_spec(shape):
    return pl.BlockSpec(shape, lambda i: (0,) * len(shape), pipeline_mode=pl.Buffered(1))


def _mix_in_body(x_ref, g_ref, win_ref, cw_ref, wa_ref,
                 qz_ref, k_ref, vz_ref, ma_ref, gbs_ref, ubuf, *, tm, tiles_per_seq):
    i = pl.program_id(0)
    h = _rms(x_ref[...], g_ref[...]).astype(BF16)

    p = _dot(h, win_ref[:, 0:3 * D_CONV])
    u = p[:, 2 * D_CONV:3 * D_CONV] * p[:, 0:D_CONV]

    @pl.when(i % tiles_per_seq == 0)
    def _():
        ubuf[0:8, :] = jnp.zeros((8, D_CONV), F32)

    ubuf[8:8 + tm, :] = u
    cw = cw_ref[...]
    conv = (cw[0:1, :] * ubuf[6:6 + tm, :] + cw[1:2, :] * ubuf[7:7 + tm, :]) + cw[2:3, :] * u
    ubuf[0:8, :] = ubuf[tm:tm + 8, :]
    ua = (p[:, D_CONV:2 * D_CONV] * conv).astype(BF16)

    o = 3 * D_CONV
    qkv = _dot(h, win_ref[:, o:o + 3 * D_ATTN])
    o += 3 * D_ATTN
    ga = _dot(h, win_ref[:, o:o + D_MODEL])
    gb = _dot(h, win_ref[:, o + D_MODEL:o + 2 * D_MODEL])

    first = lax.broadcasted_iota(jnp.int32, (tm, LANES), 1) < HEAD_DIM
    q_scale = HEAD_DIM ** -0.5
    for pr in range(N_PAIRS):
        qs = qkv[:, pr * LANES:(pr + 1) * LANES] * q_scale
        vs = qkv[:, 2 * D_ATTN + pr * LANES:2 * D_ATTN + (pr + 1) * LANES]
        lo, mid, hi = 2 * pr * LANES, (2 * pr + 1) * LANES, (2 * pr + 2) * LANES
        qz_ref[:, lo:mid] = jnp.where(first, qs, 0.0).astype(BF16)
        qz_ref[:, mid:hi] = jnp.where(first, 0.0, qs).astype(BF16)
        vz_ref[:, lo:mid] = jnp.where(first, vs, 0.0).astype(BF16)
        vz_ref[:, mid:hi] = jnp.where(first, 0.0, vs).astype(BF16)
    k_ref[...] = qkv[:, D_ATTN:2 * D_ATTN].astype(BF16)
    gbs_ref[...] = jax.nn.sigmoid(gb).astype(BF16)
    ma_ref[...] = (jax.nn.sigmoid(ga) * _dot(ua, wa_ref[...])).astype(BF16)


def _mix_in(x2, g, win, cw, wa, seq):
    t = x2.shape[0]
    tm = min(1024, seq)
    n_in = win.shape[1]
    body = functools.partial(_mix_in_body, tm=tm, tiles_per_seq=seq // tm)
    row = lambda w: pl.BlockSpec((tm, w), lambda i: (i, 0))
    return pl.pallas_call(
        body,
        grid=(t // tm,),
        in_specs=[row(D_MODEL), _const_spec((1, D_MODEL)), _const_spec((D_MODEL, n_in)),
                  _const_spec((CONV_WIDTH, D_CONV)), _const_spec((D_CONV, D_MODEL))],
        out_specs=[row(D_SPLIT), row(D_ATTN), row(D_SPLIT), row(D_MODEL), row(D_MODEL)],
        out_shape=[jax.ShapeDtypeStruct((t, D_SPLIT), BF16),
                   jax.ShapeDtypeStruct((t, D_ATTN), BF16),
                   jax.ShapeDtypeStruct((t, D_SPLIT), BF16),
                   jax.ShapeDtypeStruct((t, D_MODEL), BF16),
                   jax.ShapeDtypeStruct((t, D_MODEL), BF16)],
        scratch_shapes=[pltpu.VMEM((tm + 8, D_CONV), F32)],
        compiler_params=_params(1),
        name="mix_in",
    )(x2, g, win, cw, wa)


def _attn_body(qz_ref, k_ref, vz_ref, o_ref, r_scr, o_scr, m_scr):
    i = pl.program_id(1)
    tb = ATTN_BLOCK
    row2 = lax.broadcasted_iota(jnp.int32, (2 * tb, tb), 0)
    col2 = lax.broadcasted_iota(jnp.int32, (2 * tb, tb), 1)
    causal2 = col2 < jnp.where(row2 >= tb, row2 - tb, row2)
    rj = lax.broadcasted_iota(jnp.int32, (tb, 2 * tb), 0)
    cs_ = lax.broadcasted_iota(jnp.int32, (tb, 2 * tb), 1)
    suffix = jnp.where(jnp.logical_or(rj > cs_, cs_ >= tb), 1.0, 0.0).astype(BF16)

    def sweep(steps, groups, fresh):
        chains = [(g, p) for g in groups for p in range(N_PAIRS)]
        start = {(g, s): pl.multiple_of((ATTN_GROUP * i + g - n) * tb, tb)
                 for g in groups for s, n in enumerate(steps)}
        diag = [isinstance(n, int) and n == 0 for n in steps]
        zs, log_betas, log_1ms, cs = {}, {}, {}, {}
        rs = {ci: (None if fresh else r_scr[g * N_PAIRS + p]) for ci, (g, p) in enumerate(chains)}
        os_ = {ci: (None if fresh else o_scr[g * N_PAIRS + p]) for ci, (g, p) in enumerate(chains)}

        def scores(s):
            for ci, (g, p) in enumerate(chains):
                rows = slice(g * tb, (g + 1) * tb)
                q2 = jnp.concatenate(
                    [qz_ref[0, rows, 2 * p * LANES:(2 * p + 1) * LANES],
                     qz_ref[0, rows, (2 * p + 1) * LANES:(2 * p + 2) * LANES]], axis=0)
                kp = k_ref[0, pl.ds(start[g, s], tb), p * LANES:(p + 1) * LANES]
                zs[s, ci] = lax.dot_general(q2, kp, (((1,), (1,)), ((), ())),
                                            preferred_element_type=F32)

        def logs_and_suffix_sums(s):
            parts = []
            for ci in range(len(chains)):
                z = zs[s, ci]
                sp = jnp.log(1.0 + jnp.exp(-jnp.abs(z)))
                log_beta = jnp.minimum(z, 0.0) - sp
                log_1m = log_beta - z
                if diag[s]:
                    log_1m = jnp.where(causal2, log_1m, 0.0)
                log_betas[s, ci] = log_beta
                parts.append(log_1m.astype(BF16))
            cs[s] = _dot(jnp.concatenate(parts, axis=0), suffix)

        def weights_and_values(s):
            for ci, (g, p) in enumerate(chains):
                c = cs[s][ci * 2 * tb:(ci + 1) * 2 * tb]
                r = rs[ci]
                t = log_betas[s, ci] + c[:, 0:tb]
                a = jnp.exp(t if r is None else t + r)
                if diag[s]:
                    a = jnp.where(causal2, a, 0.0)
                rs[ci] = c[:, tb:2 * tb] if r is None else r + c[:, tb:2 * tb]
                a2 = jnp.concatenate([a[0:tb], a[tb:2 * tb]], axis=1).astype(BF16)
                v2 = jnp.concatenate(
                    [vz_ref[0, pl.ds(start[g, s], tb), 2 * p * LANES:(2 * p + 1) * LANES],
                     vz_ref[0, pl.ds(start[g, s], tb), (2 * p + 1) * LANES:(2 * p + 2) * LANES]],
                    axis=0)
                pv = _dot(a2, v2)
                os_[ci] = pv if os_[ci] is None else os_[ci] + pv

        scores(0)
        for s in range(len(steps)):
            if s + 1 < len(steps):
                scores(s + 1)
            logs_and_suffix_sums(s)
            if s >= 1:
                weights_and_values(s - 1)
        weights_and_values(len(steps) - 1)

        rmax = None
        for ci, (g, p) in enumerate(chains):
            r_scr[g * N_PAIRS + p] = rs[ci]
            o_scr[g * N_PAIRS + p] = os_[ci]
            rmax = rs[ci] if rmax is None else jnp.maximum(rmax, rs[ci])
        return jnp.max(rmax, axis=0, keepdims=True)[0, 0]

    all_groups = list(range(ATTN_GROUP))
    fused = list(range(min(ATTN_FUSED_STEPS, ATTN_GROUP + 1)))

    def tails(first_extra):
        for extra in range(first_extra, ATTN_GROUP):
            sweep([ATTN_GROUP * i + extra], all_groups[extra:], False)

    @pl.when(i == 0)
    def _():
        m_scr[0] = sweep([0], all_groups, True)

        @pl.when(m_scr[0] > LOG_F32_ZERO)
        def _():
            tails(1)

    @pl.when(i > 0)
    def _():
        m_scr[0] = sweep(fused, all_groups, True)

    def cond(c):
        n, m = c
        return jnp.logical_and(n <= ATTN_GROUP * i, m > LOG_F32_ZERO)

    def body(c):
        n, _ = c
        return n + 1, sweep([n], all_groups, False)

    n_end, m_end = lax.while_loop(cond, body, (jnp.int32(len(fused)), m_scr[0]))

    @pl.when(jnp.logical_and(i > 0, jnp.logical_and(n_end == ATTN_GROUP * i + 1,
                                                     m_end > LOG_F32_ZERO)))
    def _():
        tails(1)

    for g in range(ATTN_GROUP):
        for p in range(N_PAIRS):
            o_ref[0, g * tb:(g + 1) * tb, p * LANES:(p + 1) * LANES] = (
                o_scr[g * N_PAIRS + p].astype(o_ref.dtype))


def _attention(qz, k, vz):
    b, s, _ = k.shape
    tb = ATTN_BLOCK
    tq = ATTN_GROUP * tb
    n_chains = ATTN_GROUP * N_PAIRS
    return pl.pallas_call(
        _attn_body,
        grid=(b, s // tq),
        in_specs=[pl.BlockSpec((1, tq, D_SPLIT), lambda bi, i: (bi, i, 0)),
                  pl.BlockSpec((1, s, D_ATTN), lambda bi, i: (bi, 0, 0)),
                  pl.BlockSpec((1, s, D_SPLIT), lambda bi, i: (bi, 0, 0))],
        out_specs=pl.BlockSpec((1, tq, D_ATTN), lambda bi, i: (bi, i, 0)),
        out_shape=jax.ShapeDtypeStruct((b, s, D_ATTN), BF16),
        scratch_shapes=[pltpu.VMEM((n_chains, 2 * tb, tb), F32),
                        pltpu.VMEM((n_chains, tb, LANES), F32),
                        pltpu.SMEM((1,), F32)],
        compiler_params=_params(2),
        name="stickbreak_attn",
    )(qz, k, vz)


def _merge(x_ref, ma_ref, gbs_ref, ub_ref, wb_ref, wo_ref, g_ref):
    mix = ma_ref[...].astype(F32) + gbs_ref[...].astype(F32) * _dot(ub_ref[...], wb_ref[...])
    xn = x_ref[...] + _dot(mix.astype(BF16), wo_ref[...])
    return xn, _rms(xn, g_ref[...])


def _mix_out_dense_body(x_ref, ma_ref, gbs_ref, ub_ref, wb_ref, wo_ref, g_ref,
                        wg_ref, wu_ref, wd_ref, o_ref):
    xn, h = _merge(x_ref, ma_ref, gbs_ref, ub_ref, wb_ref, wo_ref, g_ref)
    h = h.astype(BF16)
    act = (jax.nn.silu(_dot(h, wg_ref[...])) * _dot(h, wu_ref[...])).astype(BF16)
    o_ref[...] = xn + _dot(act, wd_ref[...])


def _mix_out_router_body(x_ref, ma_ref, gbs_ref, ub_ref, wb_ref, wo_ref, g_ref, wr_ref,
                         xo_ref, h_ref, lg_ref, *, tm):
    xn, h = _merge(x_ref, ma_ref, gbs_ref, ub_ref, wb_ref, wo_ref, g_ref)
    xo_ref[...] = xn
    for c in range(ROW_CHUNKS):
        h_ref[pl.ds(c, tm, stride=ROW_CHUNKS), :] = h[:, c * LANES:(c + 1) * LANES]
    h_hi = h.astype(BF16)
    h_lo = (h - h_hi.astype(F32)).astype(BF16)
    c2 = _dot(h_hi, wr_ref[...])
    lg_ref[...] = (c2[:, 0:ROUTER_PAD] + c2[:, ROUTER_PAD:2 * ROUTER_PAD]
                   + _dot(h_lo, wr_ref[:, 0:ROUTER_PAD]))


def _mix_out_specs(tm):
    row = lambda w: pl.BlockSpec((tm, w), lambda i: (i, 0))
    return [row(D_MODEL), row(D_MODEL), row(D_MODEL), row(D_ATTN),
            _const_spec((D_ATTN, D_MODEL)), _const_spec((D_MODEL, D_MODEL)),
            _const_spec((1, D_MODEL))]


def _mix_out_dense(x2, ma, gbs, ub, wb, wo, g, wg, wu, wd):
    t = x2.shape[0]
    tm = min(512, t)
    dff = wg.shape[1]
    return pl.pallas_call(
        _mix_out_dense_body,
        grid=(t // tm,),
        in_specs=_mix_out_specs(tm) + [_const_spec((D_MODEL, dff)), _const_spec((D_MODEL, dff)),
                                       _const_spec((dff, D_MODEL))],
        out_specs=pl.BlockSpec((tm, D_MODEL), lambda i: (i, 0)),
        out_shape=jax.ShapeDtypeStruct((t, D_MODEL), F32),
        compiler_params=_params(1),
        name="mix_out_dense_ffn",
    )(x2, ma, gbs, ub, wb, wo, g, wg, wu, wd)


def _mix_out_router(x2, ma, gbs, ub, wb, wo, g, wr):
    t = x2.shape[0]
    tm = min(512, t)
    row = lambda w: pl.BlockSpec((tm, w), lambda i: (i, 0))
    return pl.pallas_call(
        functools.partial(_mix_out_router_body, tm=tm),
        grid=(t // tm,),
        in_specs=_mix_out_specs(tm) + [_const_spec((D_MODEL, 2 * ROUTER_PAD))],
        out_specs=[row(D_MODEL), pl.BlockSpec((tm * ROW_CHUNKS, LANES), lambda i: (i, 0)),
                   row(ROUTER_PAD)],
        out_shape=[jax.ShapeDtypeStruct((t, D_MODEL), F32),
                   jax.ShapeDtypeStruct((t * ROW_CHUNKS, LANES), F32),
                   jax.ShapeDtypeStruct((t, ROUTER_PAD), F32)],
        compiler_params=_params(1),
        name="mix_out_router",
    )(x2, ma, gbs, ub, wb, wo, g, wr)


def _route(logits, tm):
    t = logits.shape[0]
    top_vals, top_idx = lax.top_k(logits, TOP_K)
    e_flat = top_idx.T.reshape(-1).astype(jnp.int32)
    experts = jnp.arange(N_EXPERTS, dtype=jnp.int32)
    onehot = (e_flat[:, None] == experts[None, :]).astype(jnp.int32)
    csum = jnp.cumsum(onehot, axis=0)
    counts = csum[-1]
    tiles_e = (counts + tm - 1) // tm
    tile_end = jnp.cumsum(tiles_e)
    tile_begin = tile_end - tiles_e
    n_tiles = (TOP_K * t) // tm + N_EXPERTS
    ti = jnp.arange(n_tiles, dtype=jnp.int32)
    tile_expert = jnp.minimum(
        jnp.sum((ti[:, None] >= tile_end[None, :]).astype(jnp.int32), axis=1), N_EXPERTS - 1)
    row_of = jnp.sum((csum - 1 + (tile_begin * tm)[None, :]) * onehot, axis=1).astype(jnp.int32)
    n_valid = tile_end[-1]
    tail = n_valid + experts
    fill_tiles = jnp.concatenate([jnp.where(tiles_e > 0, tile_end - 1, -1),
                                  jnp.where(tail < n_tiles, tail, -1)]).astype(jnp.int32)
    return (top_vals, row_of, tile_expert.astype(jnp.int32), fill_tiles,
            n_valid.astype(jnp.int32).reshape(1), n_tiles)


def _row_tile(r):
    return pl.ds(pl.multiple_of(r * ROW_CHUNKS, ROW_CHUNKS), ROW_CHUNKS)


ROW_DMA_UNROLL = 32


def _row_dma_loop(n_rows, issue):
    def body(g, c):
        for j in range(ROW_DMA_UNROLL):
            issue(g * ROW_DMA_UNROLL + j, j % 2)
        return c
    lax.fori_loop(0, n_rows // ROW_DMA_UNROLL, body, 0)


DISPATCH_BUFFERS = 3


def _dispatch_body(row_ref, fill_ref, h_ref, xs_ref, hbuf, zbuf, lsem, ssem, zsem,
                   *, tm, t, n_steps):
    i = pl.program_id(0)
    nb = DISPATCH_BUFFERS
    slot = i % nb
    rc = ROW_CHUNKS
    tile = tm * rc

    def load(step, sl):
        return pltpu.make_async_copy(
            h_ref.at[pl.ds(pl.multiple_of(step * tile, tile), tile)], hbuf.at[sl], lsem.at[sl])

    def wait_rows(sl):
        for _ in range(TOP_K):
            pltpu.make_async_copy(hbuf.at[sl], xs_ref.at[pl.ds(0, tile)], ssem.at[sl]).wait()

    @pl.when(i == 0)
    def _():
        load(0, 0).start()
        zbuf[...] = jnp.zeros_like(zbuf)

        def fill(j):
            first_row = pl.multiple_of(jnp.maximum(fill_ref[j], 0) * tile, tile)
            return pltpu.make_async_copy(zbuf, xs_ref.at[pl.ds(first_row, tile)], zsem)

        for j in range(2 * N_EXPERTS):
            pl.when(fill_ref[j] >= 0)(fill(j).start)
        for j in range(2 * N_EXPERTS):
            pl.when(fill_ref[j] >= 0)(fill(j).wait)

    @pl.when(i >= 2)
    def _():
        wait_rows((i - 2) % nb)

    @pl.when(i + 1 < n_steps)
    def _():
        load(i + 1, (i + 1) % nb).start()

    load(i, slot).wait()

    def issue(r, queue):
        for k in range(TOP_K):
            pltpu.make_async_copy(hbuf.at[slot, _row_tile(r)],
                                  xs_ref.at[_row_tile(row_ref[k * t + i * tm + r])],
                                  ssem.at[slot]).start(priority=(queue + k) % 2)
    _row_dma_loop(tm, issue)

    @pl.when(i == n_steps - 1)
    def _():
        @pl.when(i >= 1)
        def _():
            wait_rows((i - 1) % nb)
        wait_rows(slot)


def _dispatch(row_of, fill_tiles, h, tm, n_tiles):
    t = h.shape[0] // ROW_CHUNKS
    n_steps = t // tm
    return pl.pallas_call(
        functools.partial(_dispatch_body, tm=tm, t=t, n_steps=n_steps),
        grid_spec=pltpu.PrefetchScalarGridSpec(
            num_scalar_prefetch=2,
            grid=(n_steps,),
            in_specs=[pl.BlockSpec(memory_space=pl.ANY)],
            out_specs=pl.BlockSpec(memory_space=pl.ANY),
            scratch_shapes=[pltpu.VMEM((DISPATCH_BUFFERS, tm * ROW_CHUNKS, LANES), F32),
                            pltpu.VMEM((tm * ROW_CHUNKS, LANES), F32),
                            pltpu.SemaphoreType.DMA((DISPATCH_BUFFERS,)),
                            pltpu.SemaphoreType.DMA((DISPATCH_BUFFERS,)),
                            pltpu.SemaphoreType.DMA],
        ),
        out_shape=jax.ShapeDtypeStruct((n_tiles * tm * ROW_CHUNKS, LANES), F32),
        compiler_params=_params(1),
        name="moe_dispatch",
    )(row_of, fill_tiles, h)


def _expert_body(te_ref, nv_ref, xs_ref, wg_ref, wu_ref, wd_ref, y_ref, *, tm):
    del te_ref
    i = pl.program_id(0)
    rc = ROW_CHUNKS

    @pl.when(i < nv_ref[0])
    def _():
        x = jnp.concatenate(
            [xs_ref[pl.ds(c, tm, stride=rc), :].astype(BF16) for c in range(rc)], axis=1)
        act = (jax.nn.silu(_dot(x, wg_ref[0])) * _dot(x, wu_ref[0])).astype(BF16)
        y = _dot(act, wd_ref[0])
        for c in range(rc):
            y_ref[pl.ds(c, tm, stride=rc), :] = y[:, c * LANES:(c + 1) * LANES]

    @pl.when(i >= nv_ref[0])
    def _():
        y_ref[...] = jnp.zeros_like(y_ref)


def _experts(tile_expert, n_valid, xs, wg, wu, wd, tm, n_tiles):
    dfe = wg.shape[2]
    wmap = lambda i, te, nv: (te[i], 0, 0)
    row = pl.BlockSpec((tm * ROW_CHUNKS, LANES), lambda i, te, nv: (i, 0))
    return pl.pallas_call(
        functools.partial(_expert_body, tm=tm),
        grid_spec=pltpu.PrefetchScalarGridSpec(
            num_scalar_prefetch=2,
            grid=(n_tiles,),
            in_specs=[row,
                      pl.BlockSpec((1, D_MODEL, dfe), wmap),
                      pl.BlockSpec((1, D_MODEL, dfe), wmap),
                      pl.BlockSpec((1, dfe, D_MODEL), wmap)],
            out_specs=row,
        ),
        out_shape=jax.ShapeDtypeStruct((n_tiles * tm * ROW_CHUNKS, LANES), F32),
        compiler_params=_params(1),
        name="moe_experts",
    )(tile_expert, n_valid, xs, wg, wu, wd)


def _combine_body(row_ref, x_ref, v_ref, g_ref, y_ref, o_ref, ybuf, sem,
                  *, tm, t, n_steps, final_norm):
    i = pl.program_id(0)
    slot = i % 2
    rc = ROW_CHUNKS

    def gather_start(step, sl):
        def issue(r, queue):
            for k in range(TOP_K):
                pltpu.make_async_copy(y_ref.at[_row_tile(row_ref[k * t + step * tm + r])],
                                      ybuf.at[sl, _row_tile(k * tm + r)],
                                      sem.at[sl]).start(priority=(queue + k) % 2)
        _row_dma_loop(tm, issue)

    @pl.when(i == 0)
    def _():
        gather_start(0, 0)

    @pl.when(i + 1 < n_steps)
    def _():
        gather_start(i + 1, 1 - slot)

    pltpu.make_async_copy(y_ref.at[pl.ds(0, TOP_K * tm * rc)], ybuf.at[slot], sem.at[slot]).wait()
    rows = lambda k: jnp.concatenate(
        [ybuf[slot, pl.ds(k * tm * rc + c, tm, stride=rc), :] for c in range(rc)], axis=1)
    v = v_ref[...]
    e = jnp.exp(v - jnp.max(v, axis=-1, keepdims=True))
    w = e / jnp.sum(e, axis=-1, keepdims=True)
    xn = x_ref[...] + (w[:, 0:1] * rows(0) + w[:, 1:2] * rows(1))
    o_ref[...] = _rms(xn, g_ref[...]) if final_norm else xn


def _combine(row_of, x2, y, top_vals, g, final_norm):
    t = x2.shape[0]
    tm = min(512, t)
    n_steps = t // tm
    return pl.pallas_call(
        functools.partial(_combine_body, tm=tm, t=t, n_steps=n_steps, final_norm=final_norm),
        grid_spec=pltpu.PrefetchScalarGridSpec(
            num_scalar_prefetch=1,
            grid=(n_steps,),
            in_specs=[pl.BlockSpec((tm, D_MODEL), lambda i, ro: (i, 0)),
                      pl.BlockSpec((tm, TOP_K), lambda i, ro: (i, 0)),
                      pl.BlockSpec((1, D_MODEL), lambda i, ro: (0, 0)),
                      pl.BlockSpec(memory_space=pl.ANY)],
            out_specs=pl.BlockSpec((tm, D_MODEL), lambda i, ro: (i, 0)),
            scratch_shapes=[pltpu.VMEM((2, TOP_K * tm * ROW_CHUNKS, LANES), F32),
                            pltpu.SemaphoreType.DMA((2,))],
        ),
        out_shape=jax.ShapeDtypeStruct((t, D_MODEL), F32),
        compiler_params=_params(1),
        name="moe_combine",
    )(row_of, x2, top_vals, g, y)


def _final_norm_body(x_ref, g_ref, o_ref):
    o_ref[...] = _rms(x_ref[...], g_ref[...])


def _final_norm(x2, g):
    t = x2.shape[0]
    tm = min(512, t)
    return pl.pallas_call(
        _final_norm_body,
        grid=(t // tm,),
        in_specs=[pl.BlockSpec((tm, D_MODEL), lambda i: (i, 0)),
                  pl.BlockSpec((1, D_MODEL), lambda i: (0, 0))],
        out_specs=pl.BlockSpec((tm, D_MODEL), lambda i: (i, 0)),
        out_shape=jax.ShapeDtypeStruct((t, D_MODEL), F32),
        compiler_params=_params(1),
        name="final_norm",
    )(x2, g)


def _router_weights(w):
    wp = jnp.pad(w, ((0, 0), (0, ROUTER_PAD - N_EXPERTS)))
    hi = wp.astype(BF16)
    lo = (wp - hi.astype(F32)).astype(BF16)
    return jnp.concatenate([hi, lo], axis=1)


def kernel(x, g_mix, w_in, conv_w, w_branch_conv, w_branch_attn, w_out, g_ffn, w_ffn_gate, w_ffn_up, w_ffn_down, w_router, w_exp_gate, w_exp_up, w_exp_down, g_final):
    b, s, d = x.shape
    depth = g_mix.shape[0]
    t = b * s
    x2 = x.reshape(t, d)
    gf = g_final.reshape(1, d)
    for i in range(depth):
        last = i == depth - 1
        moe = i % 2 == 1
        j = i // 2
        qz, k, vz, ma, gbs = _mix_in(x2, g_mix[i].reshape(1, d), w_in[i].astype(BF16), conv_w[i],
                                     w_branch_conv[i].astype(BF16), s)
        ub = _attention(qz.reshape(b, s, D_SPLIT), k.reshape(b, s, D_ATTN),
                        vz.reshape(b, s, D_SPLIT)).reshape(t, D_ATTN)
        wb = w_branch_attn[i].astype(BF16)
        wo = w_out[i].astype(BF16)
        gi = g_ffn[i].reshape(1, d)
        if not moe:
            x2 = _mix_out_dense(x2, ma, gbs, ub, wb, wo, gi, w_ffn_gate[j].astype(BF16),
                                w_ffn_up[j].astype(BF16), w_ffn_down[j].astype(BF16))
            if last:
                x2 = _final_norm(x2, gf)
        else:
            x2, h, logits = _mix_out_router(x2, ma, gbs, ub, wb, wo, gi,
                                            _router_weights(w_router[j]))
            tm = min(512, t)
            top_vals, row_of, tile_expert, fill_tiles, n_valid, n_tiles = _route(
                logits[:, :N_EXPERTS], tm)
            xs = _dispatch(row_of, fill_tiles, h, tm, n_tiles)
            y = _experts(tile_expert, n_valid, xs,
                         w_exp_gate[j].astype(BF16), w_exp_up[j].astype(BF16),
                         w_exp_down[j].astype(BF16), tm, n_tiles)
            x2 = _combine(row_of, x2, y, top_vals, gf, final_norm=last)
    return x2.reshape(b, s, d)
```

```python
import functools

import jax
import jax.numpy as jnp
from jax import lax
from jax.experimental import pallas as pl
from jax.experimental.pallas import tpu as pltpu

F32 = jnp.float32
BF16 = jnp.bfloat16

D_MODEL = 1024
D_CONV = 512
D_ATTN = 512
N_HEADS = 8
HEAD_DIM = 64
N_EXPERTS = 8
TOP_K = 2
EPS = 1e-6
CONV_WIDTH = 3

LANES = 128
VMEM_LIMIT_BYTES = 56 * 1024 * 1024

ATTN_BLOCK = 128
ATTN_GROUP = 4
ATTN_FUSED_STEPS = 3
N_PAIRS = N_HEADS // 2
D_SPLIT = 2 * D_ATTN
LOG_F32_ZERO = -104.0
ROUTER_PAD = 128
ROW_CHUNKS = D_MODEL // LANES


def _params(n_grid, vmem=VMEM_LIMIT_BYTES):
    return pltpu.CompilerParams(
        dimension_semantics=("arbitrary",) * n_grid, vmem_limit_bytes=vmem)


def _rms(x, g):
    return x * lax.rsqrt(jnp.mean(x * x, axis=-1, keepdims=True) + EPS) * g


def _dot(a, b):
    return jnp.dot(a, b, preferred_element_type=F32)


def _const_spec(shape):
    return pl.BlockSpec(shape, lambda i: (0,) * len(shape), pipeline_mode=pl.Buffered(1))


def _mix_in_body(x_ref, g_ref, win_ref, cw_ref, wa_ref,
                 qz_ref, k_ref, vz_ref, ma_ref, gbs_ref, ubuf, *, tm, tiles_per_seq):
    i = pl.program_id(0)
    h = _rms(x_ref[...], g_ref[...]).astype(BF16)

    p = _dot(h, win_ref[:, 0:3 * D_CONV])
    u = p[:, 2 * D_CONV:3 * D_CONV] * p[:, 0:D_CONV]

    @pl.when(i % tiles_per_seq == 0)
    def _():
        ubuf[0:8, :] = jnp.zeros((8, D_CONV), F32)

    ubuf[8:8 + tm, :] = u
    cw = cw_ref[...]
    conv = (cw[0:1, :] * ubuf[6:6 + tm, :] + cw[1:2, :] * ubuf[7:7 + tm, :]) + cw[2:3, :] * u
    ubuf[0:8, :] = ubuf[tm:tm + 8, :]
    ua = (p[:, D_CONV:2 * D_CONV] * conv).astype(BF16)

    o = 3 * D_CONV
    qkv = _dot(h, win_ref[:, o:o + 3 * D_ATTN])
    o += 3 * D_ATTN
    ga = _dot(h, win_ref[:, o:o + D_MODEL])
    gb = _dot(h, win_ref[:, o + D_MODEL:o + 2 * D_MODEL])

    first = lax.broadcasted_iota(jnp.int32, (tm, LANES), 1) < HEAD_DIM
    q_scale = HEAD_DIM ** -0.5
    for pr in range(N_PAIRS):
        qs = qkv[:, pr * LANES:(pr + 1) * LANES] * q_scale
        vs = qkv[:, 2 * D_ATTN + pr * LANES:2 * D_ATTN + (pr + 1) * LANES]
        lo, mid, hi = 2 * pr * LANES, (2 * pr + 1) * LANES, (2 * pr + 2) * LANES
        qz_ref[:, lo:mid] = jnp.where(first, qs, 0.0).astype(BF16)
        qz_ref[:, mid:hi] = jnp.where(first, 0.0, qs).astype(BF16)
        vz_ref[:, lo:mid] = jnp.where(first, vs, 0.0).astype(BF16)
        vz_ref[:, mid:hi] = jnp.where(first, 0.0, vs).astype(BF16)
    k_ref[...] = qkv[:, D_ATTN:2 * D_ATTN].astype(BF16)
    gbs_ref[...] = jax.nn.sigmoid(gb).astype(BF16)
    ma_ref[...] = (jax.nn.sigmoid(ga) * _dot(ua, wa_ref[...])).astype(BF16)


def _mix_in(x2, g, win, cw, wa, seq):
    t = x2.shape[0]
    tm = min(1024, seq)
    n_in = win.shape[1]
    body = functools.partial(_mix_in_body, tm=tm, tiles_per_seq=seq // tm)
    row = lambda w: pl.BlockSpec((tm, w), lambda i: (i, 0))
    return pl.pallas_call(
        body,
        grid=(t // tm,),
        in_specs=[row(D_MODEL), _const_spec((1, D_MODEL)), _const_spec((D_MODEL, n_in)),
                  _const_spec((CONV_WIDTH, D_CONV)), _const_spec((D_CONV, D_MODEL))],
        out_specs=[row(D_SPLIT), row(D_ATTN), row(D_SPLIT), row(D_MODEL), row(D_MODEL)],
        out_shape=[jax.ShapeDtypeStruct((t, D_SPLIT), BF16),
                   jax.ShapeDtypeStruct((t, D_ATTN), BF16),
                   jax.ShapeDtypeStruct((t, D_SPLIT), BF16),
                   jax.ShapeDtypeStruct((t, D_MODEL), BF16),
                   jax.ShapeDtypeStruct((t, D_MODEL), BF16)],
        scratch_shapes=[pltpu.VMEM((tm + 8, D_CONV), F32)],
        compiler_params=_params(1),
        name="mix_in",
    )(x2, g, win, cw, wa)


def _attn_body(qz_ref, k_ref, vz_ref, o_ref, r_scr, o_scr, m_scr):
    i = pl.program_id(1)
    tb = ATTN_BLOCK
    row2 = lax.broadcasted_iota(jnp.int32, (2 * tb, tb), 0)
    col2 = lax.broadcasted_iota(jnp.int32, (2 * tb, tb), 1)
    causal2 = col2 < jnp.where(row2 >= tb, row2 - tb, row2)
    rj = lax.broadcasted_iota(jnp.int32, (tb, 2 * tb), 0)
    cs_ = lax.broadcasted_iota(jnp.int32, (tb, 2 * tb), 1)
    suffix = jnp.where(jnp.logical_or(rj > cs_, cs_ >= tb), 1.0, 0.0).astype(BF16)

    def sweep(steps, groups, fresh):
        chains = [(g, p) for g in groups for p in range(N_PAIRS)]
        start = {(g, s): pl.multiple_of((ATTN_GROUP * i + g - n) * tb, tb)
                 for g in groups for s, n in enumerate(steps)}
        diag = [isinstance(n, int) and n == 0 for n in steps]
        zs, log_betas, log_1ms, cs = {}, {}, {}, {}
        rs = {ci: (None if fresh else r_scr[g * N_PAIRS + p]) for ci, (g, p) in enumerate(chains)}
        os_ = {ci: (None if fresh else o_scr[g * N_PAIRS + p]) for ci, (g, p) in enumerate(chains)}

        def scores(s):
            for ci, (g, p) in enumerate(chains):
                rows = slice(g * tb, (g + 1) * tb)
                q2 = jnp.concatenate(
                    [qz_ref[0, rows, 2 * p * LANES:(2 * p + 1) * LANES],
                     qz_ref[0, rows, (2 * p + 1) * LANES:(2 * p + 2) * LANES]], axis=0)
                kp = k_ref[0, pl.ds(start[g, s], tb), p * LANES:(p + 1) * LANES]
                zs[s, ci] = lax.dot_general(q2, kp, (((1,), (1,)), ((), ())),
                                            preferred_element_type=F32)

        def logs_and_suffix_sums(s):
            parts = []
            for ci in range(len(chains)):
                z = zs[s, ci]
                sp = jnp.log(1.0 + jnp.exp(-jnp.abs(z)))
                log_beta = jnp.minimum(z, 0.0) - sp
                log_1m = log_beta - z
                if diag[s]:
                    log_1m = jnp.where(causal2, log_1m, 0.0)
                log_betas[s, ci] = log_beta
                parts.append(log_1m.astype(BF16))
            cs[s] = _dot(jnp.concatenate(parts, axis=0), suffix)

        def weights_and_values(s):
            for ci, (g, p) in enumerate(chains):
                c = cs[s][ci * 2 * tb:(ci + 1) * 2 * tb]
                r = rs[ci]
                t = log_betas[s, ci] + c[:, 0:tb]
                a = jnp.exp(t if r is None else t + r)
                if diag[s]:
                    a = jnp.where(causal2, a, 0.0)
                rs[ci] = c[:, tb:2 * tb] if r is None else r + c[:, tb:2 * tb]
                a2 = jnp.concatenate([a[0:tb], a[tb:2 * tb]], axis=1).astype(BF16)
                v2 = jnp.concatenate(
                    [vz_ref[0, pl.ds(start[g, s], tb), 2 * p * LANES:(2 * p + 1) * LANES],
                     vz_ref[0, pl.ds(start[g, s], tb), (2 * p + 1) * LANES:(2 * p + 2) * LANES]],
                    axis=0)
                pv = _dot(a2, v2)
                os_[ci] = pv if os_[ci] is None else os_[ci] + pv

        scores(0)
        for s in range(len(steps)):
            if s + 1 < len(steps):
                scores(s + 1)
            logs_and_suffix_sums(s)
            if s >= 1:
                weights_and_values(s - 1)
        weights_and_values(len(steps) - 1)

        rmax = None
        for ci, (g, p) in enumerate(chains):
            r_scr[g * N_PAIRS + p] = rs[ci]
            o_scr[g * N_PAIRS + p] = os_[ci]
            rmax = rs[ci] if rmax is None else jnp.maximum(rmax, rs[ci])
        return jnp.max(rmax, axis=0, keepdims=True)[0, 0]

    all_groups = list(range(ATTN_GROUP))
    fused = list(range(min(ATTN_FUSED_STEPS, ATTN_GROUP + 1)))

    def tails(first_extra):
        for extra in range(first_extra, ATTN_GROUP):
            sweep([ATTN_GROUP * i + extra], all_groups[extra:], False)

    @pl.when(i == 0)
    def _():
        m_scr[0] = sweep([0], all_groups, True)

        @pl.when(m_scr[0] > LOG_F32_ZERO)
        def _():
            tails(1)

    @pl.when(i > 0)
    def _():
        m_scr[0] = sweep(fused, all_groups, True)

    def cond(c):
        n, m = c
        return jnp.logical_and(n <= ATTN_GROUP * i, m > LOG_F32_ZERO)

    def body(c):
        n, _ = c
        return n + 1, sweep([n], all_groups, False)

    n_end, m_end = lax.while_loop(cond, body, (jnp.int32(len(fused)), m_scr[0]))

    @pl.when(jnp.logical_and(i > 0, jnp.logical_and(n_end == ATTN_GROUP * i + 1,
                                                     m_end > LOG_F32_ZERO)))
    def _():
        tails(1)

    for g in range(ATTN_GROUP):
        for p in range(N_PAIRS):
            o_ref[0, g * tb:(g + 1) * tb, p * LANES:(p + 1) * LANES] = (
                o_scr[g * N_PAIRS + p].astype(o_ref.dtype))


def _attention(qz, k, vz):
    b, s, _ = k.shape
    tb = ATTN_BLOCK
    tq = ATTN_GROUP * tb
    n_chains = ATTN_GROUP * N_PAIRS
    return pl.pallas_call(
        _attn_body,
        grid=(b, s // tq),
        in_specs=[pl.BlockSpec((1, tq, D_SPLIT), lambda bi, i: (bi, i, 0)),
                  pl.BlockSpec((1, s, D_ATTN), lambda bi, i: (bi, 0, 0)),
                  pl.BlockSpec((1, s, D_SPLIT), lambda bi, i: (bi, 0, 0))],
        out_specs=pl.BlockSpec((1, tq, D_ATTN), lambda bi, i: (bi, i, 0)),
        out_shape=jax.ShapeDtypeStruct((b, s, D_ATTN), BF16),
        scratch_shapes=[pltpu.VMEM((n_chains, 2 * tb, tb), F32),
                        pltpu.VMEM((n_chains, tb, LANES), F32),
                        pltpu.SMEM((1,), F32)],
        compiler_params=_params(2),
        name="stickbreak_attn",
    )(qz, k, vz)


def _merge(x_ref, ma_ref, gbs_ref, ub_ref, wb_ref, wo_ref, g_ref):
    mix = ma_ref[...].astype(F32) + gbs_ref[...].astype(F32) * _dot(ub_ref[...], wb_ref[...])
    xn = x_ref[...] + _dot(mix.astype(BF16), wo_ref[...])
    return xn, _rms(xn, g_ref[...])


def _mix_out_dense_body(x_ref, ma_ref, gbs_ref, ub_ref, wb_ref, wo_ref, g_ref,
                        wg_ref, wu_ref, wd_ref, o_ref):
    xn, h = _merge(x_ref, ma_ref, gbs_ref, ub_ref, wb_ref, wo_ref, g_ref)
    h = h.astype(BF16)
    act = (jax.nn.silu(_dot(h, wg_ref[...])) * _dot(h, wu_ref[...])).astype(BF16)
    o_ref[...] = xn + _dot(act, wd_ref[...])


def _mix_out_router_body(x_ref, ma_ref, gbs_ref, ub_ref, wb_ref, wo_ref, g_ref, wr_ref,
                         xo_ref, h_ref, lg_ref, *, tm):
    xn, h = _merge(x_ref, ma_ref, gbs_ref, ub_ref, wb_ref, wo_ref, g_ref)
    xo_ref[...] = xn
    for c in range(ROW_CHUNKS):
        h_ref[pl.ds(c, tm, stride=ROW_CHUNKS), :] = h[:, c * LANES:(c + 1) * LANES]
    h_hi = h.astype(BF16)
    h_lo = (h - h_hi.astype(F32)).astype(BF16)
    c2 = _dot(h_hi, wr_ref[...])
    lg_ref[...] = (c2[:, 0:ROUTER_PAD] + c2[:, ROUTER_PAD:2 * ROUTER_PAD]
                   + _dot(h_lo, wr_ref[:, 0:ROUTER_PAD]))


def _mix_out_specs(tm):
    row = lambda w: pl.BlockSpec((tm, w), lambda i: (i, 0))
    return [row(D_MODEL), row(D_MODEL), row(D_MODEL), row(D_ATTN),
            _const_spec((D_ATTN, D_MODEL)), _const_spec((D_MODEL, D_MODEL)),
            _const_spec((1, D_MODEL))]


def _mix_out_dense(x2, ma, gbs, ub, wb, wo, g, wg, wu, wd):
    t = x2.shape[0]
    tm = min(512, t)
    dff = wg.shape[1]
    return pl.pallas_call(
        _mix_out_dense_body,
        grid=(t // tm,),
        in_specs=_mix_out_specs(tm) + [_const_spec((D_MODEL, dff)), _const_spec((D_MODEL, dff)),
                                       _const_spec((dff, D_MODEL))],
        out_specs=pl.BlockSpec((tm, D_MODEL), lambda i: (i, 0)),
        out_shape=jax.ShapeDtypeStruct((t, D_MODEL), F32),
        compiler_params=_params(1),
        name="mix_out_dense_ffn",
    )(x2, ma, gbs, ub, wb, wo, g, wg, wu, wd)


def _mix_out_router(x2, ma, gbs, ub, wb, wo, g, wr):
    t = x2.shape[0]
    tm = min(1024, t)
    row = lambda w: pl.BlockSpec((tm, w), lambda i: (i, 0))
    return pl.pallas_call(
        functools.partial(_mix_out_router_body, tm=tm),
        grid=(t // tm,),
        in_specs=_mix_out_specs(tm) + [_const_spec((D_MODEL, 2 * ROUTER_PAD))],
        out_specs=[row(D_MODEL), pl.BlockSpec((tm * ROW_CHUNKS, LANES), lambda i: (i, 0)),
                   row(ROUTER_PAD)],
        out_shape=[jax.ShapeDtypeStruct((t, D_MODEL), F32),
                   jax.ShapeDtypeStruct((t * ROW_CHUNKS, LANES), F32),
                   jax.ShapeDtypeStruct((t, ROUTER_PAD), F32)],
        compiler_params=_params(1),
        name="mix_out_router",
    )(x2, ma, gbs, ub, wb, wo, g, wr)


def _route(logits, tm):
    t = logits.shape[0]
    top_vals, top_idx = lax.top_k(logits, TOP_K)
    e_flat = top_idx.T.reshape(-1).astype(jnp.int32)
    experts = jnp.arange(N_EXPERTS, dtype=jnp.int32)
    onehot = (e_flat[:, None] == experts[None, :]).astype(jnp.int32)
    csum = jnp.cumsum(onehot, axis=0)
    counts = csum[-1]
    tiles_e = (counts + tm - 1) // tm
    tile_end = jnp.cumsum(tiles_e)
    tile_begin = tile_end - tiles_e
    n_tiles = (TOP_K * t) // tm + N_EXPERTS
    ti = jnp.arange(n_tiles, dtype=jnp.int32)
    tile_expert = jnp.minimum(
        jnp.sum((ti[:, None] >= tile_end[None, :]).astype(jnp.int32), axis=1), N_EXPERTS - 1)
    row_of = jnp.sum((csum - 1 + (tile_begin * tm)[None, :]) * onehot, axis=1).astype(jnp.int32)
    n_valid = tile_end[-1]
    tail = n_valid + experts
    fill_tiles = jnp.concatenate([jnp.where(tiles_e > 0, tile_end - 1, -1),
                                  jnp.where(tail < n_tiles, tail, -1)]).astype(jnp.int32)
    return (top_vals, row_of, tile_expert.astype(jnp.int32), fill_tiles,
            n_valid.astype(jnp.int32).reshape(1), n_tiles)


def _row_tile(r):
    return pl.ds(pl.multiple_of(r * ROW_CHUNKS, ROW_CHUNKS), ROW_CHUNKS)


ROW_DMA_UNROLL = 32


def _row_dma_loop(n_rows, issue):
    def body(g, c):
        for j in range(ROW_DMA_UNROLL):
            issue(g * ROW_DMA_UNROLL + j, j % 2)
        return c
    lax.fori_loop(0, n_rows // ROW_DMA_UNROLL, body, 0)


DISPATCH_BUFFERS = 3


def _dispatch_body(row_ref, fill_ref, h_ref, xs_ref, hbuf, zbuf, lsem, ssem, zsem,
                   *, tm, t, n_steps):
    i = pl.program_id(0)
    nb = DISPATCH_BUFFERS
    slot = i % nb
    rc = ROW_CHUNKS
    tile = tm * rc

    def load(step, sl):
        return pltpu.make_async_copy(
            h_ref.at[pl.ds(pl.multiple_of(step * tile, tile), tile)], hbuf.at[sl], lsem.at[sl])

    def wait_rows(sl):
        for _ in range(TOP_K):
            pltpu.make_async_copy(hbuf.at[sl], xs_ref.at[pl.ds(0, tile)], ssem.at[sl]).wait()

    @pl.when(i == 0)
    def _():
        load(0, 0).start()
        zbuf[...] = jnp.zeros_like(zbuf)

        def fill(j):
            first_row = pl.multiple_of(jnp.maximum(fill_ref[j], 0) * tile, tile)
            return pltpu.make_async_copy(zbuf, xs_ref.at[pl.ds(first_row, tile)], zsem)

        for j in range(2 * N_EXPERTS):
            pl.when(fill_ref[j] >= 0)(fill(j).start)
        for j in range(2 * N_EXPERTS):
            pl.when(fill_ref[j] >= 0)(fill(j).wait)

    @pl.when(i >= 2)
    def _():
        wait_rows((i - 2) % nb)

    @pl.when(i + 1 < n_steps)
    def _():
        load(i + 1, (i + 1) % nb).start()

    load(i, slot).wait()

    def issue(r, queue):
        for k in range(TOP_K):
            pltpu.make_async_copy(hbuf.at[slot, _row_tile(r)],
                                  xs_ref.at[_row_tile(row_ref[k * t + i * tm + r])],
                                  ssem.at[slot]).start(priority=(queue + k) % 2)
    _row_dma_loop(tm, issue)

    @pl.when(i == n_steps - 1)
    def _():
        @pl.when(i >= 1)
        def _():
            wait_rows((i - 1) % nb)
        wait_rows(slot)


def _dispatch(row_of, fill_tiles, h, tm, n_tiles):
    t = h.shape[0] // ROW_CHUNKS
    n_steps = t // tm
    return pl.pallas_call(
        functools.partial(_dispatch_body, tm=tm, t=t, n_steps=n_steps),
        grid_spec=pltpu.PrefetchScalarGridSpec(
            num_scalar_prefetch=2,
            grid=(n_steps,),
            in_specs=[pl.BlockSpec(memory_space=pl.ANY)],
            out_specs=pl.BlockSpec(memory_space=pl.ANY),
            scratch_shapes=[pltpu.VMEM((DISPATCH_BUFFERS, tm * ROW_CHUNKS, LANES), F32),
                            pltpu.VMEM((tm * ROW_CHUNKS, LANES), F32),
                            pltpu.SemaphoreType.DMA((DISPATCH_BUFFERS,)),
                            pltpu.SemaphoreType.DMA((DISPATCH_BUFFERS,)),
                            pltpu.SemaphoreType.DMA],
        ),
        out_shape=jax.ShapeDtypeStruct((n_tiles * tm * ROW_CHUNKS, LANES), F32),
        compiler_params=_params(1),
        name="moe_dispatch",
    )(row_of, fill_tiles, h)


def _expert_body(te_ref, nv_ref, xs_ref, wg_ref, wu_ref, wd_ref, y_ref, *, tm):
    del te_ref
    i = pl.program_id(0)
    rc = ROW_CHUNKS

    @pl.when(i < nv_ref[0])
    def _():
        x = jnp.concatenate(
            [xs_ref[pl.ds(c, tm, stride=rc), :].astype(BF16) for c in range(rc)], axis=1)
        act = (jax.nn.silu(_dot(x, wg_ref[0])) * _dot(x, wu_ref[0])).astype(BF16)
        y = _dot(act, wd_ref[0])
        for c in range(rc):
            y_ref[pl.ds(c, tm, stride=rc), :] = y[:, c * LANES:(c + 1) * LANES]

    @pl.when(i >= nv_ref[0])
    def _():
        y_ref[...] = jnp.zeros_like(y_ref)


def _experts(tile_expert, n_valid, xs, wg, wu, wd, tm, n_tiles):
    dfe = wg.shape[2]
    wmap = lambda i, te, nv: (te[i], 0, 0)
    row = pl.BlockSpec((tm * ROW_CHUNKS, LANES), lambda i, te, nv: (i, 0))
    return pl.pallas_call(
        functools.partial(_expert_body, tm=tm),
        grid_spec=pltpu.PrefetchScalarGridSpec(
            num_scalar_prefetch=2,
            grid=(n_tiles,),
            in_specs=[row,
                      pl.BlockSpec((1, D_MODEL, dfe), wmap),
                      pl.BlockSpec((1, D_MODEL, dfe), wmap),
                      pl.BlockSpec((1, dfe, D_MODEL), wmap)],
            out_specs=row,
        ),
        out_shape=jax.ShapeDtypeStruct((n_tiles * tm * ROW_CHUNKS, LANES), F32),
        compiler_params=_params(1),
        name="moe_experts",
    )(tile_expert, n_valid, xs, wg, wu, wd)


def _combine_body(row_ref, x_ref, v_ref, g_ref, y_ref, o_ref, ybuf, sem,
                  *, tm, t, n_steps, final_norm):
    i = pl.program_id(0)
    slot = i % 2
    rc = ROW_CHUNKS

    def gather_start(step, sl):
        def issue(r, queue):
            for k in range(TOP_K):
                pltpu.make_async_copy(y_ref.at[_row_tile(row_ref[k * t + step * tm + r])],
                                      ybuf.at[sl, _row_tile(k * tm + r)],
                                      sem.at[sl]).start(priority=(queue + k) % 2)
        _row_dma_loop(tm, issue)

    @pl.when(i == 0)
    def _():
        gather_start(0, 0)

    @pl.when(i + 1 < n_steps)
    def _():
        gather_start(i + 1, 1 - slot)

    pltpu.make_async_copy(y_ref.at[pl.ds(0, TOP_K * tm * rc)], ybuf.at[slot], sem.at[slot]).wait()
    rows = lambda k: jnp.concatenate(
        [ybuf[slot, pl.ds(k * tm * rc + c, tm, stride=rc), :] for c in range(rc)], axis=1)
    v = v_ref[...]
    e = jnp.exp(v - jnp.max(v, axis=-1, keepdims=True))
    w = e / jnp.sum(e, axis=-1, keepdims=True)
    xn = x_ref[...] + (w[:, 0:1] * rows(0) + w[:, 1:2] * rows(1))
    o_ref[...] = _rms(xn, g_ref[...]) if final_norm else xn


def _combine(row_of, x2, y, top_vals, g, final_norm):
    t = x2.shape[0]
    tm = min(1024, t)
    n_steps = t // tm
    return pl.pallas_call(
        functools.partial(_combine_body, tm=tm, t=t, n_steps=n_steps, final_norm=final_norm),
        grid_spec=pltpu.PrefetchScalarGridSpec(
            num_scalar_prefetch=1,
            grid=(n_steps,),
            in_specs=[pl.BlockSpec((tm, D_MODEL), lambda i, ro: (i, 0)),
                      pl.BlockSpec((tm, TOP_K), lambda i, ro: (i, 0)),
                      pl.BlockSpec((1, D_MODEL), lambda i, ro: (0, 0)),
                      pl.BlockSpec(memory_space=pl.ANY)],
            out_specs=pl.BlockSpec((tm, D_MODEL), lambda i, ro: (i, 0)),
            scratch_shapes=[pltpu.VMEM((2, TOP_K * tm * ROW_CHUNKS, LANES), F32),
                            pltpu.SemaphoreType.DMA((2,))],
        ),
        out_shape=jax.ShapeDtypeStruct((t, D_MODEL), F32),
        compiler_params=_params(1),
        name="moe_combine",
    )(row_of, x2, top_vals, g, y)


def _final_norm_body(x_ref, g_ref, o_ref):
    o_ref[...] = _rms(x_ref[...], g_ref[...])


def _final_norm(x2, g):
    t = x2.shape[0]
    tm = min(512, t)
    return pl.pallas_call(
        _final_norm_body,
        grid=(t // tm,),
        in_specs=[pl.BlockSpec((tm, D_MODEL), lambda i: (i, 0)),
                  pl.BlockSpec((1, D_MODEL), lambda i: (0, 0))],
        out_specs=pl.BlockSpec((tm, D_MODEL), lambda i: (i, 0)),
        out_shape=jax.ShapeDtypeStruct((t, D_MODEL), F32),
        compiler_params=_params(1),
        name="final_norm",
    )(x2, g)


def _router_weights(w):
    wp = jnp.pad(w, ((0, 0), (0, ROUTER_PAD - N_EXPERTS)))
    hi = wp.astype(BF16)
    lo = (wp - hi.astype(F32)).astype(BF16)
    return jnp.concatenate([hi, lo], axis=1)


def kernel(x, g_mix, w_in, conv_w, w_branch_conv, w_branch_attn, w_out, g_ffn, w_ffn_gate, w_ffn_up, w_ffn_down, w_router, w_exp_gate, w_exp_up, w_exp_down, g_final):
    b, s, d = x.shape
    depth = g_mix.shape[0]
    t = b * s
    x2 = x.reshape(t, d)
    gf = g_final.reshape(1, d)
    for i in range(depth):
        last = i == depth - 1
        moe = i % 2 == 1
        j = i // 2
        qz, k, vz, ma, gbs = _mix_in(x2, g_mix[i].reshape(1, d), w_in[i].astype(BF16), conv_w[i],
                                     w_branch_conv[i].astype(BF16), s)
        ub = _attention(qz.reshape(b, s, D_SPLIT), k.reshape(b, s, D_ATTN),
                        vz.reshape(b, s, D_SPLIT)).reshape(t, D_ATTN)
        wb = w_branch_attn[i].astype(BF16)
        wo = w_out[i].astype(BF16)
        gi = g_ffn[i].reshape(1, d)
        if not moe:
            x2 = _mix_out_dense(x2, ma, gbs, ub, wb, wo, gi, w_ffn_gate[j].astype(BF16),
                                w_ffn_up[j].astype(BF16), w_ffn_down[j].astype(BF16))
            if last:
                x2 = _final_norm(x2, gf)
        else:
            x2, h, logits = _mix_out_router(x2, ma, gbs, ub, wb, wo, gi,
                                            _router_weights(w_router[j]))
            tm = min(512, t)
            top_vals, row_of, tile_expert, fill_tiles, n_valid, n_tiles = _route(
                logits[:, :N_EXPERTS], tm)
            xs = _dispatch(row_of, fill_tiles, h, tm, n_tiles)
            y = _experts(tile_expert, n_valid, xs,
                         w_exp_gate[j].astype(BF16), w_exp_up[j].astype(BF16),
                         w_exp_down[j].astype(BF16), tm, n_tiles)
            x2 = _combine(row_of, x2, y, top_vals, gf, final_norm=last)
    return x2.reshape(b, s, d)
```

```python
import functools

import jax
import jax.numpy as jnp
from jax import lax
from jax.experimental import pallas as pl
from jax.experimental.pallas import tpu as pltpu

F32 = jnp.float32
BF16 = jnp.bfloat16

D_MODEL = 1024
D_CONV = 512
D_ATTN = 512
N_HEADS = 8
HEAD_DIM = 64
N_EXPERTS = 8
TOP_K = 2
EPS = 1e-6
CONV_WIDTH = 3

LANES = 128
VMEM_LIMIT_BYTES = 56 * 1024 * 1024

ATTN_BLOCK = 128
ATTN_GROUP = 4
ATTN_FUSED_STEPS = 3
N_PAIRS = N_HEADS // 2
D_SPLIT = 2 * D_ATTN
LOG_F32_ZERO = -104.0
ROUTER_PAD = 128
ROW_CHUNKS = D_MODEL // LANES


def _params(n_grid, vmem=VMEM_LIMIT_BYTES):
    return pltpu.CompilerParams(
        dimension_semantics=("arbitrary",) * n_grid, vmem_limit_bytes=vmem)


def _rms(x, g):
    return x * lax.rsqrt(jnp.mean(x * x, axis=-1, keepdims=True) + EPS) * g


def _dot(a, b):
    return jnp.dot(a, b, preferred_element_type=F32)


def _const_spec(shape):
    return pl.BlockSpec(shape, lambda i: (0,) * len(shape), pipeline_mode=pl.Buffered(1))


def _mix_in_body(x_ref, g_ref, win_ref, cw_ref, wa_ref,
                 qz_ref, k_ref, vz_ref, ma_ref, gbs_ref, ubuf, *, tm, tiles_per_seq):
    i = pl.program_id(0)
    h = _rms(x_ref[...], g_ref[...]).astype(BF16)

    p = _dot(h, win_ref[:, 0:3 * D_CONV])
    u = p[:, 2 * D_CONV:3 * D_CONV] * p[:, 0:D_CONV]

    @pl.when(i % tiles_per_seq == 0)
    def _():
        ubuf[0:8, :] = jnp.zeros((8, D_CONV), F32)

    ubuf[8:8 + tm, :] = u
    cw = cw_ref[...]
    conv = (cw[0:1, :] * ubuf[6:6 + tm, :] + cw[1:2, :] * ubuf[7:7 + tm, :]) + cw[2:3, :] * u
    ubuf[0:8, :] = ubuf[tm:tm + 8, :]
    ua = (p[:, D_CONV:2 * D_CONV] * conv).astype(BF16)

    o = 3 * D_CONV
    qkv = _dot(h, win_ref[:, o:o + 3 * D_ATTN])
    o += 3 * D_ATTN
    ga = _dot(h, win_ref[:, o:o + D_MODEL])
    gb = _dot(h, win_ref[:, o + D_MODEL:o + 2 * D_MODEL])

    first = lax.broadcasted_iota(jnp.int32, (tm, LANES), 1) < HEAD_DIM
    q_scale = HEAD_DIM ** -0.5
    for pr in range(N_PAIRS):
        qs = qkv[:, pr * LANES:(pr + 1) * LANES] * q_scale
        vs = qkv[:, 2 * D_ATTN + pr * LANES:2 * D_ATTN + (pr + 1) * LANES]
        lo, mid, hi = 2 * pr * LANES, (2 * pr + 1) * LANES, (2 * pr + 2) * LANES
        qz_ref[:, lo:mid] = jnp.where(first, qs, 0.0).astype(BF16)
        qz_ref[:, mid:hi] = jnp.where(first, 0.0, qs).astype(BF16)
        vz_ref[:, lo:mid] = jnp.where(first, vs, 0.0).astype(BF16)
        vz_ref[:, mid:hi] = jnp.where(first, 0.0, vs).astype(BF16)
    k_ref[...] = qkv[:, D_ATTN:2 * D_ATTN].astype(BF16)
    gbs_ref[...] = jax.nn.sigmoid(gb).astype(BF16)
    ma_ref[...] = (jax.nn.sigmoid(ga) * _dot(ua, wa_ref[...])).astype(BF16)


def _mix_in(x2, g, win, cw, wa, seq):
    t = x2.shape[0]
    tm = min(1024, seq)
    n_in = win.shape[1]
    body = functools.partial(_mix_in_body, tm=tm, tiles_per_seq=seq // tm)
    row = lambda w: pl.BlockSpec((tm, w), lambda i: (i, 0))
    return pl.pallas_call(
        body,
        grid=(t // tm,),
        in_specs=[row(D_MODEL), _const_spec((1, D_MODEL)), _const_spec((D_MODEL, n_in)),
                  _const_spec((CONV_WIDTH, D_CONV)), _const_spec((D_CONV, D_MODEL))],
        out_specs=[row(D_SPLIT), row(D_ATTN), row(D_SPLIT), row(D_MODEL), row(D_MODEL)],
        out_shape=[jax.ShapeDtypeStruct((t, D_SPLIT), BF16),
                   jax.ShapeDtypeStruct((t, D_ATTN), BF16),
                   jax.ShapeDtypeStruct((t, D_SPLIT), BF16),
                   jax.ShapeDtypeStruct((t, D_MODEL), BF16),
                   jax.ShapeDtypeStruct((t, D_MODEL), BF16)],
        scratch_shapes=[pltpu.VMEM((tm + 8, D_CONV), F32)],
        compiler_params=_params(1),
        name="mix_in",
    )(x2, g, win, cw, wa)


def _attn_body(qz_ref, k_ref, vz_ref, o_ref, r_scr, o_scr, m_scr):
    i = pl.program_id(1)
    tb = ATTN_BLOCK
    row2 = lax.broadcasted_iota(jnp.int32, (2 * tb, tb), 0)
    col2 = lax.broadcasted_iota(jnp.int32, (2 * tb, tb), 1)
    causal2 = col2 < jnp.where(row2 >= tb, row2 - tb, row2)
    rj = lax.broadcasted_iota(jnp.int32, (tb, 2 * tb), 0)
    cs_ = lax.broadcasted_iota(jnp.int32, (tb, 2 * tb), 1)
    suffix = jnp.where(jnp.logical_or(rj > cs_, cs_ >= tb), 1.0, 0.0).astype(BF16)

    def sweep(steps, groups, fresh):
        chains = [(g, p) for g in groups for p in range(N_PAIRS)]
        start = {(g, s): pl.multiple_of((ATTN_GROUP * i + g - n) * tb, tb)
                 for g in groups for s, n in enumerate(steps)}
        diag = [isinstance(n, int) and n == 0 for n in steps]
        zs, log_betas, log_1ms, cs = {}, {}, {}, {}
        rs = {ci: (None if fresh else r_scr[g * N_PAIRS + p]) for ci, (g, p) in enumerate(chains)}
        os_ = {ci: (None if fresh else o_scr[g * N_PAIRS + p]) for ci, (g, p) in enumerate(chains)}

        def scores(s):
            for ci, (g, p) in enumerate(chains):
                rows = slice(g * tb, (g + 1) * tb)
                q2 = jnp.concatenate(
                    [qz_ref[0, rows, 2 * p * LANES:(2 * p + 1) * LANES],
                     qz_ref[0, rows, (2 * p + 1) * LANES:(2 * p + 2) * LANES]], axis=0)
                kp = k_ref[0, pl.ds(start[g, s], tb), p * LANES:(p + 1) * LANES]
                zs[s, ci] = lax.dot_general(q2, kp, (((1,), (1,)), ((), ())),
                                            preferred_element_type=F32)

        def logs_and_suffix_sums(s):
            parts = []
            for ci in range(len(chains)):
                z = zs[s, ci]
                sp = jnp.log(1.0 + jnp.exp(-jnp.abs(z)))
                log_beta = jnp.minimum(z, 0.0) - sp
                log_1m = log_beta - z
                if diag[s]:
                    log_1m = jnp.where(causal2, log_1m, 0.0)
                log_betas[s, ci] = log_beta
                parts.append(log_1m.astype(BF16))
            cs[s] = _dot(jnp.concatenate(parts, axis=0), suffix)

        def weights_and_values(s):
            for ci, (g, p) in enumerate(chains):
                c = cs[s][ci * 2 * tb:(ci + 1) * 2 * tb]
                r = rs[ci]
                t = log_betas[s, ci] + c[:, 0:tb]
                a = jnp.exp(t if r is None else t + r)
                if diag[s]:
                    a = jnp.where(causal2, a, 0.0)
                rs[ci] = c[:, tb:2 * tb] if r is None else r + c[:, tb:2 * tb]
                a2 = jnp.concatenate([a[0:tb], a[tb:2 * tb]], axis=1).astype(BF16)
                v2 = jnp.concatenate(
                    [vz_ref[0, pl.ds(start[g, s], tb), 2 * p * LANES:(2 * p + 1) * LANES],
                     vz_ref[0, pl.ds(start[g, s], tb), (2 * p + 1) * LANES:(2 * p + 2) * LANES]],
                    axis=0)
                pv = _dot(a2, v2)
                os_[ci] = pv if os_[ci] is None else os_[ci] + pv

        scores(0)
        for s in range(len(steps)):
            if s + 1 < len(steps):
                scores(s + 1)
            logs_and_suffix_sums(s)
            if s >= 1:
                weights_and_values(s - 1)
        weights_and_values(len(steps) - 1)

        rmax = None
        for ci, (g, p) in enumerate(chains):
            r_scr[g * N_PAIRS + p] = rs[ci]
            o_scr[g * N_PAIRS + p] = os_[ci]
            rmax = rs[ci] if rmax is None else jnp.maximum(rmax, rs[ci])
        return jnp.max(rmax, axis=0, keepdims=True)[0, 0]

    all_groups = list(range(ATTN_GROUP))
    fused = list(range(min(ATTN_FUSED_STEPS, ATTN_GROUP + 1)))

    def tails(first_extra):
        for extra in range(first_extra, ATTN_GROUP):
            sweep([ATTN_GROUP * i + extra], all_groups[extra:], False)

    @pl.when(i == 0)
    def _():
        m_scr[0] = sweep([0], all_groups, True)

        @pl.when(m_scr[0] > LOG_F32_ZERO)
        def _():
            tails(1)

    @pl.when(i > 0)
    def _():
        m_scr[0] = sweep(fused, all_groups, True)

    def cond(c):
        n, m = c
        return jnp.logical_and(n <= ATTN_GROUP * i, m > LOG_F32_ZERO)

    def body(c):
        n, _ = c
        return n + 1, sweep([n], all_groups, False)

    n_end, m_end = lax.while_loop(cond, body, (jnp.int32(len(fused)), m_scr[0]))

    @pl.when(jnp.logical_and(i > 0, jnp.logical_and(n_end == ATTN_GROUP * i + 1,
                                                     m_end > LOG_F32_ZERO)))
    def _():
        tails(1)

    for g in range(ATTN_GROUP):
        for p in range(N_PAIRS):
            o_ref[0, g * tb:(g + 1) * tb, p * LANES:(p + 1) * LANES] = (
                o_scr[g * N_PAIRS + p].astype(o_ref.dtype))


def _attention(qz, k, vz):
    b, s, _ = k.shape
    tb = ATTN_BLOCK
    tq = ATTN_GROUP * tb
    n_chains = ATTN_GROUP * N_PAIRS
    return pl.pallas_call(
        _attn_body,
        grid=(b, s // tq),
        in_specs=[pl.BlockSpec((1, tq, D_SPLIT), lambda bi, i: (bi, i, 0)),
                  pl.BlockSpec((1, s, D_ATTN), lambda bi, i: (bi, 0, 0)),
                  pl.BlockSpec((1, s, D_SPLIT), lambda bi, i: (bi, 0, 0))],
        out_specs=pl.BlockSpec((1, tq, D_ATTN), lambda bi, i: (bi, i, 0)),
        out_shape=jax.ShapeDtypeStruct((b, s, D_ATTN), BF16),
        scratch_shapes=[pltpu.VMEM((n_chains, 2 * tb, tb), F32),
                        pltpu.VMEM((n_chains, tb, LANES), F32),
                        pltpu.SMEM((1,), F32)],
        compiler_params=_params(2),
        name="stickbreak_attn",
    )(qz, k, vz)


def _merge(x_ref, ma_ref, gbs_ref, ub_ref, wb_ref, wo_ref, g_ref):
    mix = ma_ref[...].astype(F32) + gbs_ref[...].astype(F32) * _dot(ub_ref[...], wb_ref[...])
    xn = x_ref[...] + _dot(mix.astype(BF16), wo_ref[...])
    return xn, _rms(xn, g_ref[...])


def _mix_out_dense_body(x_ref, ma_ref, gbs_ref, ub_ref, wb_ref, wo_ref, g_ref,
                        wg_ref, wu_ref, wd_ref, o_ref):
    xn, h = _merge(x_ref, ma_ref, gbs_ref, ub_ref, wb_ref, wo_ref, g_ref)
    h = h.astype(BF16)
    act = (jax.nn.silu(_dot(h, wg_ref[...])) * _dot(h, wu_ref[...])).astype(BF16)
    o_ref[...] = xn + _dot(act, wd_ref[...])


def _mix_out_router_body(x_ref, ma_ref, gbs_ref, ub_ref, wb_ref, wo_ref, g_ref, wr_ref,
                         xo_ref, h_ref, lg_ref, *, tm):
    xn, h = _merge(x_ref, ma_ref, gbs_ref, ub_ref, wb_ref, wo_ref, g_ref)
    xo_ref[...] = xn
    for c in range(ROW_CHUNKS):
        h_ref[pl.ds(c, tm, stride=ROW_CHUNKS), :] = h[:, c * LANES:(c + 1) * LANES]
    h_hi = h.astype(BF16)
    h_lo = (h - h_hi.astype(F32)).astype(BF16)
    c2 = _dot(h_hi, wr_ref[...])
    lg_ref[...] = (c2[:, 0:ROUTER_PAD] + c2[:, ROUTER_PAD:2 * ROUTER_PAD]
                   + _dot(h_lo, wr_ref[:, 0:ROUTER_PAD]))


def _mix_out_specs(tm):
    row = lambda w: pl.BlockSpec((tm, w), lambda i: (i, 0))
    return [row(D_MODEL), row(D_MODEL), row(D_MODEL), row(D_ATTN),
            _const_spec((D_ATTN, D_MODEL)), _const_spec((D_MODEL, D_MODEL)),
            _const_spec((1, D_MODEL))]


def _mix_out_dense(x2, ma, gbs, ub, wb, wo, g, wg, wu, wd):
    t = x2.shape[0]
    tm = min(512, t)
    dff = wg.shape[1]
    return pl.pallas_call(
        _mix_out_dense_body,
        grid=(t // tm,),
        in_specs=_mix_out_specs(tm) + [_const_spec((D_MODEL, dff)), _const_spec((D_MODEL, dff)),
                                       _const_spec((dff, D_MODEL))],
        out_specs=pl.BlockSpec((tm, D_MODEL), lambda i: (i, 0)),
        out_shape=jax.ShapeDtypeStruct((t, D_MODEL), F32),
        compiler_params=_params(1),
        name="mix_out_dense_ffn",
    )(x2, ma, gbs, ub, wb, wo, g, wg, wu, wd)


def _mix_out_router(x2, ma, gbs, ub, wb, wo, g, wr):
    t = x2.shape[0]
    tm = min(1024, t)
    row = lambda w: pl.BlockSpec((tm, w), lambda i: (i, 0))
    return pl.pallas_call(
        functools.partial(_mix_out_router_body, tm=tm),
        grid=(t // tm,),
        in_specs=_mix_out_specs(tm) + [_const_spec((D_MODEL, 2 * ROUTER_PAD))],
        out_specs=[row(D_MODEL), pl.BlockSpec((tm * ROW_CHUNKS, LANES), lambda i: (i, 0)),
                   row(ROUTER_PAD)],
        out_shape=[jax.ShapeDtypeStruct((t, D_MODEL), F32),
                   jax.ShapeDtypeStruct((t * ROW_CHUNKS, LANES), F32),
                   jax.ShapeDtypeStruct((t, ROUTER_PAD), F32)],
        compiler_params=_params(1),
        name="mix_out_router",
    )(x2, ma, gbs, ub, wb, wo, g, wr)


def _route(logits, tm):
    t = logits.shape[0]
    top_vals, top_idx = lax.top_k(logits, TOP_K)
    e_flat = top_idx.T.reshape(-1).astype(jnp.int32)
    experts = jnp.arange(N_EXPERTS, dtype=jnp.int32)
    onehot = (e_flat[:, None] == experts[None, :]).astype(jnp.int32)
    csum = jnp.cumsum(onehot, axis=0)
    counts = csum[-1]
    tiles_e = (counts + tm - 1) // tm
    tile_end = jnp.cumsum(tiles_e)
    tile_begin = tile_end - tiles_e
    n_tiles = (TOP_K * t) // tm + N_EXPERTS
    ti = jnp.arange(n_tiles, dtype=jnp.int32)
    tile_expert = jnp.minimum(
        jnp.sum((ti[:, None] >= tile_end[None, :]).astype(jnp.int32), axis=1), N_EXPERTS - 1)
    row_of = jnp.sum((csum - 1 + (tile_begin * tm)[None, :]) * onehot, axis=1).astype(jnp.int32)
    n_valid = tile_end[-1]
    tail = n_valid + experts
    fill_tiles = jnp.concatenate([jnp.where(tiles_e > 0, tile_end - 1, -1),
                                  jnp.where(tail < n_tiles, tail, -1)]).astype(jnp.int32)
    return (top_vals, row_of, tile_expert.astype(jnp.int32), fill_tiles,
            n_valid.astype(jnp.int32).reshape(1), n_tiles)


def _row_tile(r):
    return pl.ds(pl.multiple_of(r * ROW_CHUNKS, ROW_CHUNKS), ROW_CHUNKS)


ROW_DMA_UNROLL = 32


def _row_dma_loop(n_rows, issue):
    def body(g, c):
        for j in range(ROW_DMA_UNROLL):
            issue(g * ROW_DMA_UNROLL + j, j % 2)
        return c
    lax.fori_loop(0, n_rows // ROW_DMA_UNROLL, body, 0)


DISPATCH_BUFFERS = 3


def _dispatch_body(row_ref, fill_ref, h_ref, xs_ref, hbuf, zbuf, lsem, ssem, zsem,
                   *, tm, t, n_steps):
    i = pl.program_id(0)
    nb = DISPATCH_BUFFERS
    slot = i % nb
    rc = ROW_CHUNKS
    tile = tm * rc

    def load(step, sl):
        return pltpu.make_async_copy(
            h_ref.at[pl.ds(pl.multiple_of(step * tile, tile), tile)], hbuf.at[sl], lsem.at[sl])

    def wait_rows(sl):
        for _ in range(TOP_K):
            pltpu.make_async_copy(hbuf.at[sl], xs_ref.at[pl.ds(0, tile)], ssem.at[sl]).wait()

    @pl.when(i == 0)
    def _():
        load(0, 0).start()
        zbuf[...] = jnp.zeros_like(zbuf)

        def fill(j):
            first_row = pl.multiple_of(jnp.maximum(fill_ref[j], 0) * tile, tile)
            return pltpu.make_async_copy(zbuf, xs_ref.at[pl.ds(first_row, tile)], zsem)

        for j in range(2 * N_EXPERTS):
            pl.when(fill_ref[j] >= 0)(fill(j).start)
        for j in range(2 * N_EXPERTS):
            pl.when(fill_ref[j] >= 0)(fill(j).wait)

    @pl.when(i >= 2)
    def _():
        wait_rows((i - 2) % nb)

    @pl.when(i + 1 < n_steps)
    def _():
        load(i + 1, (i + 1) % nb).start()

    load(i, slot).wait()

    def issue(r, queue):
        for k in range(TOP_K):
            pltpu.make_async_copy(hbuf.at[slot, _row_tile(r)],
                                  xs_ref.at[_row_tile(row_ref[k * t + i * tm + r])],
                                  ssem.at[slot]).start(priority=(queue + k) % 2)
    _row_dma_loop(tm, issue)

    @pl.when(i == n_steps - 1)
    def _():
        @pl.when(i >= 1)
        def _():
            wait_rows((i - 1) % nb)
        wait_rows(slot)


def _dispatch(row_of, fill_tiles, h, tm, n_tiles):
    t = h.shape[0] // ROW_CHUNKS
    n_steps = t // tm
    return pl.pallas_call(
        functools.partial(_dispatch_body, tm=tm, t=t, n_steps=n_steps),
        grid_spec=pltpu.PrefetchScalarGridSpec(
            num_scalar_prefetch=2,
            grid=(n_steps,),
            in_specs=[pl.BlockSpec(memory_space=pl.ANY)],
            out_specs=pl.BlockSpec(memory_space=pl.ANY),
            scratch_shapes=[pltpu.VMEM((DISPATCH_BUFFERS, tm * ROW_CHUNKS, LANES), F32),
                            pltpu.VMEM((tm * ROW_CHUNKS, LANES), F32),
                            pltpu.SemaphoreType.DMA((DISPATCH_BUFFERS,)),
                            pltpu.SemaphoreType.DMA((DISPATCH_BUFFERS,)),
                            pltpu.SemaphoreType.DMA],
        ),
        out_shape=jax.ShapeDtypeStruct((n_tiles * tm * ROW_CHUNKS, LANES), F32),
        compiler_params=_params(1),
        name="moe_dispatch",
    )(row_of, fill_tiles, h)


def _expert_body(te_ref, nv_ref, xs_ref, wg_ref, wu_ref, wd_ref, y_ref, *, tm):
    del te_ref
    i = pl.program_id(0)
    rc = ROW_CHUNKS

    @pl.when(i < nv_ref[0])
    def _():
        x = jnp.concatenate(
            [xs_ref[pl.ds(c, tm, stride=rc), :].astype(BF16) for c in range(rc)], axis=1)
        act = (jax.nn.silu(_dot(x, wg_ref[0])) * _dot(x, wu_ref[0])).astype(BF16)
        y = _dot(act, wd_ref[0])
        for c in range(rc):
            y_ref[pl.ds(c, tm, stride=rc), :] = y[:, c * LANES:(c + 1) * LANES]

    @pl.when(i >= nv_ref[0])
    def _():
        y_ref[...] = jnp.zeros_like(y_ref)


def _experts(tile_expert, n_valid, xs, wg, wu, wd, tm, n_tiles):
    dfe = wg.shape[2]
    wmap = lambda i, te, nv: (te[i], 0, 0)
    row = pl.BlockSpec((tm * ROW_CHUNKS, LANES), lambda i, te, nv: (i, 0))
    return pl.pallas_call(
        functools.partial(_expert_body, tm=tm),
        grid_spec=pltpu.PrefetchScalarGridSpec(
            num_scalar_prefetch=2,
            grid=(n_tiles,),
            in_specs=[row,
                      pl.BlockSpec((1, D_MODEL, dfe), wmap),
                      pl.BlockSpec((1, D_MODEL, dfe), wmap),
                      pl.BlockSpec((1, dfe, D_MODEL), wmap)],
            out_specs=row,
        ),
        out_shape=jax.ShapeDtypeStruct((n_tiles * tm * ROW_CHUNKS, LANES), F32),
        compiler_params=_params(1),
        name="moe_experts",
    )(tile_expert, n_valid, xs, wg, wu, wd)


def _combine_body(row_ref, x_ref, v_ref, g_ref, y_ref, o_ref, ybuf, sem,
                  *, tm, t, n_steps, final_norm):
    i = pl.program_id(0)
    slot = i % 2
    rc = ROW_CHUNKS

    def gather_start(step, sl):
        def issue(r, queue):
            for k in range(TOP_K):
                pltpu.make_async_copy(y_ref.at[_row_tile(row_ref[k * t + step * tm + r])],
                                      ybuf.at[sl, _row_tile(k * tm + r)],
                                      sem.at[sl]).start(priority=(queue + k) % 2)
        _row_dma_loop(tm, issue)

    @pl.when(i == 0)
    def _():
        gather_start(0, 0)

    @pl.when(i + 1 < n_steps)
    def _():
        gather_start(i + 1, 1 - slot)

    pltpu.make_async_copy(y_ref.at[pl.ds(0, TOP_K * tm * rc)], ybuf.at[slot], sem.at[slot]).wait()
    rows = lambda k: jnp.concatenate(
        [ybuf[slot, pl.ds(k * tm * rc + c, tm, stride=rc), :] for c in range(rc)], axis=1)
    v = v_ref[...]
    e = jnp.exp(v - jnp.max(v, axis=-1, keepdims=True))
    w = e / jnp.sum(e, axis=-1, keepdims=True)
    xn = x_ref[...] + (w[:, 0:1] * rows(0) + w[:, 1:2] * rows(1))
    o_ref[...] = _rms(xn, g_ref[...]) if final_norm else xn


def _combine(row_of, x2, y, top_vals, g, final_norm):
    t = x2.shape[0]
    tm = min(512, t)
    n_steps = t // tm
    return pl.pallas_call(
        functools.partial(_combine_body, tm=tm, t=t, n_steps=n_steps, final_norm=final_norm),
        grid_spec=pltpu.PrefetchScalarGridSpec(
            num_scalar_prefetch=1,
            grid=(n_steps,),
            in_specs=[pl.BlockSpec((tm, D_MODEL), lambda i, ro: (i, 0)),
                      pl.BlockSpec((tm, TOP_K), lambda i, ro: (i, 0)),
                      pl.BlockSpec((1, D_MODEL), lambda i, ro: (0, 0)),
                      pl.BlockSpec(memory_space=pl.ANY)],
            out_specs=pl.BlockSpec((tm, D_MODEL), lambda i, ro: (i, 0)),
            scratch_shapes=[pltpu.VMEM((2, TOP_K * tm * ROW_CHUNKS, LANES), F32),
                            pltpu.SemaphoreType.DMA((2,))],
        ),
        out_shape=jax.ShapeDtypeStruct((t, D_MODEL), F32),
        compiler_params=_params(1),
        name="moe_combine",
    )(row_of, x2, top_vals, g, y)


def _final_norm_body(x_ref, g_ref, o_ref):
    o_ref[...] = _rms(x_ref[...], g_ref[...])


def _final_norm(x2, g):
    t = x2.shape[0]
    tm = min(512, t)
    return pl.pallas_call(
        _final_norm_body,
        grid=(t // tm,),
        in_specs=[pl.BlockSpec((tm, D_MODEL), lambda i: (i, 0)),
                  pl.BlockSpec((1, D_MODEL), lambda i: (0, 0))],
        out_specs=pl.BlockSpec((tm, D_MODEL), lambda i: (i, 0)),
        out_shape=jax.ShapeDtypeStruct((t, D_MODEL), F32),
        compiler_params=_params(1),
        name="final_norm",
    )(x2, g)


def _router_weights(w):
    wp = jnp.pad(w, ((0, 0), (0, ROUTER_PAD - N_EXPERTS)))
    hi = wp.astype(BF16)
    lo = (wp - hi.astype(F32)).astype(BF16)
    return jnp.concatenate([hi, lo], axis=1)


def kernel(x, g_mix, w_in, conv_w, w_branch_conv, w_branch_attn, w_out, g_ffn, w_ffn_gate, w_ffn_up, w_ffn_down, w_router, w_exp_gate, w_exp_up, w_exp_down, g_final):
    b, s, d = x.shape
    depth = g_mix.shape[0]
    t = b * s
    x2 = x.reshape(t, d)
    gf = g_final.reshape(1, d)
    for i in range(depth):
        last = i == depth - 1
        moe = i % 2 == 1
        j = i // 2
        qz, k, vz, ma, gbs = _mix_in(x2, g_mix[i].reshape(1, d), w_in[i].astype(BF16), conv_w[i],
                                     w_branch_conv[i].astype(BF16), s)
        ub = _attention(qz.reshape(b, s, D_SPLIT), k.reshape(b, s, D_ATTN),
                        vz.reshape(b, s, D_SPLIT)).reshape(t, D_ATTN)
        wb = w_branch_attn[i].astype(BF16)
        wo = w_out[i].astype(BF16)
        gi = g_ffn[i].reshape(1, d)
        if not moe:
            x2 = _mix_out_dense(x2, ma, gbs, ub, wb, wo, gi, w_ffn_gate[j].astype(BF16),
                                w_ffn_up[j].astype(BF16), w_ffn_down[j].astype(BF16))
            if last:
                x2 = _final_norm(x2, gf)
        else:
            x2, h, logits = _mix_out_router(x2, ma, gbs, ub, wb, wo, gi,
                                            _router_weights(w_router[j]))
            tm = min(512, t)
            top_vals, row_of, tile_expert, fill_tiles, n_valid, n_tiles = _route(
                logits[:, :N_EXPERTS], tm)
            xs = _dispatch(row_of, fill_tiles, h, tm, n_tiles)
            y = _experts(tile_expert, n_valid, xs,
                         w_exp_gate[j].astype(BF16), w_exp_up[j].astype(BF16),
                         w_exp_down[j].astype(BF16), tm, n_tiles)
            x2 = _combine(row_of, x2, y, top_vals, gf, final_norm=last)
    return x2.reshape(b, s, d)
```
